```python
import jax
import jax.numpy as jnp
from jax import lax
import numpy as np

D_MODEL = 4096
BATCH = 1
SEQ = 8192
DEPTH = 1
DEC_BATCH = 128
DEC_SEQ = 8
PAST_LEN = 2048
PAGE_SIZE = 128

MIX_WIDTH = D_MODEL
FOX_WIDTH = MIX_WIDTH // 2
RWKV_WIDTH = MIX_WIDTH - FOX_WIDTH
FOX_HEAD_DIM = 128
FOX_HEADS = FOX_WIDTH // FOX_HEAD_DIM
RWKV_HEAD_DIM = 64
RWKV_HEADS = RWKV_WIDTH // RWKV_HEAD_DIM
DECAY_LORA = max(32, int(round(1.8 * RWKV_WIDTH ** 0.5 / 32)) * 32)
AAA_LORA = max(32, int(round(1.8 * RWKV_WIDTH ** 0.5 / 32)) * 32)
GATE_LORA = max(32, int(round(0.6 * RWKV_WIDTH ** 0.8 / 32)) * 32)
FOX_SPLITS = (FOX_WIDTH, FOX_WIDTH, FOX_WIDTH, FOX_HEADS)
RWKV_SPLITS = (RWKV_WIDTH, RWKV_WIDTH, RWKV_WIDTH, DECAY_LORA, AAA_LORA, GATE_LORA)
FOX_COLS = sum(FOX_SPLITS)
RWKV_COLS = sum(RWKV_SPLITS)
FOX_SCALE = FOX_HEAD_DIM ** -0.5
Q_BLOCK = 128
MEM_TOKENS = 256
MEM_HEADS = 4
MEM_HEAD_DIM = 128
MEM_WIDTH = MEM_HEADS * MEM_HEAD_DIM
MEM_SCALE = MEM_HEAD_DIM ** -0.5
N_GROUPS = 8
EXPERTS_PER_GROUP = 8
N_EXPERTS = N_GROUPS * EXPERTS_PER_GROUP
TOP_K = 2
D_EXPERT = D_MODEL // 8
EXPERT_BLOCK = 128
NORM_EPS = 1e-6
RWKV_GN_EPS = 64e-5

kernel_name = 'fox_rwkv7_hmoe_hybrid_step'


def rms_norm(x, gain):
    xf = x.astype(jnp.float32)
    xf = xf * lax.rsqrt(jnp.mean(xf * xf, axis=-1, keepdims=True) + NORM_EPS)
    return (xf * gain.astype(jnp.float32)).astype(x.dtype)


def split_cols(z, sizes):
    return jnp.split(z, [int(i) for i in np.cumsum(sizes)[:-1]], axis=-1)


def gather_pages(pool, page_table):
    g = pool[page_table]
    return g.reshape((page_table.shape[0], page_table.shape[1] * pool.shape[1]) + pool.shape[2:])


def _fox_block(q, c_q, q_pos, segments):
    cq = jnp.swapaxes(c_q, 1, 2)[..., None]
    logits = []
    for k, _, c_k, k_pos in segments:
        s = jnp.einsum('bqhd,bkhd->bhqk', q, k, preferred_element_type=jnp.float32) * FOX_SCALE
        s = s + cq - jnp.swapaxes(c_k, 1, 2)[:, :, None, :]
        logits.append(jnp.where(k_pos[None, :] <= q_pos[:, None], s, -jnp.inf))
    p = jax.nn.softmax(jnp.concatenate(logits, axis=-1), axis=-1)
    out, off = 0.0, 0
    for k, v, _, _ in segments:
        n = k.shape[1]
        out = out + jnp.einsum('bhqk,bkhd->bqhd', p[..., off:off + n].astype(v.dtype), v,
                               preferred_element_type=jnp.float32)
        off += n
    return out.astype(q.dtype)


def fox_attention(q, c_q, q_pos, segments):
    B, T, H, D = q.shape
    if T <= Q_BLOCK or T % Q_BLOCK:
        return _fox_block(q, c_q, q_pos, segments)
    nb = T // Q_BLOCK
    qb = jnp.moveaxis(q.reshape(B, nb, Q_BLOCK, H, D), 1, 0)
    cb = jnp.moveaxis(c_q.reshape(B, nb, Q_BLOCK, H), 1, 0)
    pb = q_pos.reshape(nb, Q_BLOCK)
    out = lax.map(lambda blk: _fox_block(blk[0], blk[1], blk[2], segments), (qb, cb, pb))
    return jnp.moveaxis(out, 0, 1).reshape(B, T, H, D)


def fox_mixer(p_fox, past, q_gain, k_gain, b_f):
    B, T, _ = p_fox.shape
    q, k, v, f = split_cols(p_fox, FOX_SPLITS)
    heads = lambda t: t.reshape(B, T, FOX_HEADS, FOX_HEAD_DIM)
    q = rms_norm(heads(q), q_gain)
    k = rms_norm(heads(k), k_gain)
    v = heads(v)
    logf = jax.nn.log_sigmoid(f.astype(jnp.float32) + b_f.astype(jnp.float32))
    if past is None:
        c_new = jnp.cumsum(logf, axis=1)
        q_pos = jnp.arange(T)
        segments = ((k, v, c_new, q_pos),)
    else:
        k_past, v_past, logf_past = past
        n_past = k_past.shape[1]
        c_past = jnp.cumsum(logf_past.astype(jnp.float32), axis=1)
        c_new = c_past[:, -1:] + jnp.cumsum(logf, axis=1)
        q_pos = n_past + jnp.arange(T)
        segments = ((k_past, v_past, c_past, jnp.arange(n_past)), (k, v, c_new, q_pos))
    out = fox_attention(q, c_new, q_pos, segments)
    return out.reshape(B, T, FOX_WIDTH), k, v, logf.astype(p_fox.dtype)


def token_shift(p, prev, mu):
    p_prev = jnp.concatenate([prev[:, None, :].astype(p.dtype), p[:, :-1]], axis=1)
    return p + mu.astype(p.dtype) * (p_prev - p)


def rwkv7_scan(s0, r, decay, k, v, a_vec, b_vec):
    def step(s, inp):
        r_t, w_t, k_t, v_t, a_t, b_t = inp
        sa = jnp.einsum('bhvk,bhk->bhv', s, a_t)
        s = (s * w_t[:, :, None, :] + sa[..., None] * b_t[:, :, None, :]
             + v_t[..., None] * k_t[:, :, None, :])
        return s, jnp.einsum('bhvk,bhk->bhv', s, r_t)
    xs = tuple(jnp.moveaxis(t, 1, 0) for t in (r, decay, k, v, a_vec, b_vec))
    s_final, ys = lax.scan(step, s0, xs)
    return jnp.moveaxis(ys, 0, 1), s_final


def rwkv7_mixer(p_rwkv, shift_prev, s0, mu, w0, w2, a0, a2, g2, k_k, k_a, r_k, gn_w, gn_b):
    B, T, _ = p_rwkv.shape
    f32 = jnp.float32
    z = token_shift(p_rwkv, shift_prev, mu).astype(f32)
    r, k, v, xw, xa, xg = split_cols(z, RWKV_SPLITS)
    w_log = -jax.nn.softplus(-(w0.astype(f32) + jnp.tanh(xw) @ w2.astype(f32))) - 0.5
    decay = jnp.exp(-jnp.exp(w_log))
    a = jax.nn.sigmoid(a0.astype(f32) + xa @ a2.astype(f32))
    g = jax.nn.sigmoid(xg) @ g2.astype(f32)
    heads = lambda t: t.reshape(B, T, RWKV_HEADS, RWKV_HEAD_DIM)
    kk = heads(k * k_k.astype(f32))
    kk = kk * lax.rsqrt(jnp.maximum(jnp.sum(kk * kk, axis=-1, keepdims=True), 1e-24))
    k = k * (1.0 + (a - 1.0) * k_a.astype(f32))
    rh, kh, vh, ah = heads(r), heads(k), heads(v), heads(a)
    o, s_new = rwkv7_scan(s0.astype(f32), rh, heads(decay), kh, vh, -kk, kk * ah)
    mean = jnp.mean(o, axis=-1, keepdims=True)
    var = jnp.mean(jnp.square(o - mean), axis=-1, keepdims=True)
    o = ((o - mean) * lax.rsqrt(var + RWKV_GN_EPS)
         * gn_w.astype(f32).reshape(RWKV_HEADS, RWKV_HEAD_DIM)
         + gn_b.astype(f32).reshape(RWKV_HEADS, RWKV_HEAD_DIM))
    o = o + jnp.sum(rh * kh * r_k.astype(f32), axis=-1, keepdims=True) * vh
    o = o.reshape(B, T, RWKV_WIDTH) * g
    return o.astype(p_rwkv.dtype), s_new, p_rwkv[:, -1]


def memory_kv(mem, norm_src, w_mem_kv, k_gain):
    B, M, _ = mem.shape
    k, v = split_cols(rms_norm(mem, norm_src) @ w_mem_kv, (MEM_WIDTH, MEM_WIDTH))
    k = rms_norm(k.reshape(B, M, MEM_HEADS, MEM_HEAD_DIM), k_gain)
    return k, v.reshape(B, M, MEM_HEADS, MEM_HEAD_DIM)


def memory_cross_attention(hn, mem_k, mem_v, w_q, q_gain, w_o):
    B, T, _ = hn.shape
    q = rms_norm((hn @ w_q).reshape(B, T, MEM_HEADS, MEM_HEAD_DIM), q_gain)
    s = jnp.einsum('bqhd,bmhd->bhqm', q, mem_k, preferred_element_type=jnp.float32) * MEM_SCALE
    p = jax.nn.softmax(s, axis=-1).astype(mem_v.dtype)
    o = jnp.einsum('bhqm,bmhd->bqhd', p, mem_v)
    return o.reshape(B, T, MEM_WIDTH).astype(hn.dtype) @ w_o


def routed_experts(xf, expert_idx, weights, w_gate, w_up, w_down):
    N, D = xf.shape
    A = N * TOP_K
    flat_e = expert_idx.reshape(A).astype(jnp.int32)
    flat_tok = jnp.repeat(jnp.arange(N, dtype=jnp.int32), TOP_K)
    flat_w = weights.reshape(A)
    order = jnp.argsort(flat_e)
    e_sorted = flat_e[order]
    counts = jnp.zeros((N_EXPERTS,), jnp.int32).at[flat_e].add(1)
    padded = (counts + EXPERT_BLOCK - 1) // EXPERT_BLOCK * EXPERT_BLOCK
    pad_end = jnp.cumsum(padded)
    pad_start = pad_end - padded
    start = jnp.cumsum(counts) - counts
    dest = pad_start[e_sorted] + jnp.arange(A, dtype=jnp.int32) - start[e_sorted]
    n_blocks = -(-A // EXPERT_BLOCK) + N_EXPERTS
    L = n_blocks * EXPERT_BLOCK
    slot_tok = jnp.full((L,), N, jnp.int32).at[dest].set(flat_tok[order])
    slot_w = jnp.zeros((L,), flat_w.dtype).at[dest].set(flat_w[order])
    block_start = jnp.arange(n_blocks, dtype=jnp.int32) * EXPERT_BLOCK
    block_expert = jnp.minimum(jnp.searchsorted(pad_end, block_start, side='right'), N_EXPERTS - 1)
    x_pad = jnp.concatenate([xf, jnp.zeros((1, D), xf.dtype)], axis=0)

    def run_block(args):
        tok, e = args
        xb = x_pad[tok]
        hb = jax.nn.silu(xb @ w_gate[e]) * (xb @ w_up[e])
        return hb @ w_down[e]

    yb = lax.map(run_block, (slot_tok.reshape(n_blocks, EXPERT_BLOCK), block_expert))
    contrib = yb.reshape(L, D) * slot_w[:, None].astype(yb.dtype)
    y = jnp.zeros((N + 1, D), xf.dtype).at[slot_tok].add(contrib.astype(xf.dtype))
    return y[:N]


def hierarchical_moe(xn, w_router_group, w_router_expert, w_gate, w_up, w_down):
    B, T, D = xn.shape
    xf = xn.reshape(B * T, D)
    N = xf.shape[0]
    g_logits = (xf @ w_router_group).astype(jnp.float32)
    g_prob = jax.nn.softmax(g_logits, axis=-1)
    g_sel = jnp.argmax(g_logits, axis=-1).astype(jnp.int32)
    g_w = jnp.take_along_axis(g_prob, g_sel[:, None], axis=-1)
    e_logits = (xf @ w_router_expert).astype(jnp.float32).reshape(N, N_GROUPS, EXPERTS_PER_GROUP)
    e_logits = jnp.take_along_axis(e_logits, g_sel[:, None, None], axis=1)[:, 0]
    top_p, top_i = lax.top_k(jax.nn.softmax(e_logits, axis=-1), TOP_K)
    top_p = top_p / jnp.sum(top_p, axis=-1, keepdims=True)
    expert_idx = g_sel[:, None] * EXPERTS_PER_GROUP + top_i
    y = routed_experts(xf, expert_idx, g_w * top_p, w_gate, w_up, w_down)
    return y.reshape(B, T, D)


def decoder_layer(x, mem_k, mem_v, fox_past, rwkv_s0, shift_prev, weights):
    (norm_mix, w_in, fox_q_gain, fox_k_gain, fox_b_f, rwkv_mu, rwkv_w0, rwkv_w2, rwkv_a0,
     rwkv_a2, rwkv_g2, rwkv_k_k, rwkv_k_a, rwkv_r_k, rwkv_gn_w, rwkv_gn_b, w_out, norm_mem_q,
     w_mem_q, mem_q_gain, w_mem_o, norm_ffn, w_router_group, w_router_expert, w_exp_gate,
     w_exp_up, w_exp_down) = weights
    proj = rms_norm(x, norm_mix) @ w_in
    p_fox, p_rwkv = proj[..., :FOX_COLS], proj[..., FOX_COLS:]
    fox_out, fox_k, fox_v, fox_logf = fox_mixer(p_fox, fox_past, fox_q_gain, fox_k_gain, fox_b_f)
    rwkv_out, rwkv_s, rwkv_shift = rwkv7_mixer(p_rwkv, shift_prev, rwkv_s0, rwkv_mu, rwkv_w0,
                                               rwkv_w2, rwkv_a0, rwkv_a2, rwkv_g2, rwkv_k_k,
                                               rwkv_k_a, rwkv_r_k, rwkv_gn_w, rwkv_gn_b)
    h = x + jnp.concatenate([fox_out, rwkv_out], axis=-1) @ w_out
    h = h + memory_cross_attention(rms_norm(h, norm_mem_q), mem_k, mem_v, w_mem_q, mem_q_gain, w_mem_o)
    y = h + hierarchical_moe(rms_norm(h, norm_ffn), w_router_group, w_router_expert,
                             w_exp_gate, w_exp_up, w_exp_down)
    return y, fox_k, fox_v, fox_logf, rwkv_s.astype(x.dtype), rwkv_shift


def setup_inputs(seed: int = 0) -> dict:
    key = jax.random.key(seed)
    keys = jax.random.split(key, 48)
    ks = iter(range(48))
    f32 = jnp.float32

    def nrm(shape, scale):
        return jax.random.normal(keys[next(ks)], shape, f32) * scale

    def gain(n):
        return 1.0 + nrm((n,), 0.02)

    n_pages = PAST_LEN // PAGE_SIZE
    n_used = DEC_BATCH * n_pages
    n_pool = n_used + max(1, n_used // 4)
    x_prompt = nrm((BATCH, SEQ, D_MODEL), 1.0)
    x_sample = nrm((DEC_BATCH, DEC_SEQ, D_MODEL), 1.0)
    mem_prompt = nrm((BATCH, MEM_TOKENS, D_MODEL), 1.0)
    cache_fox_k = nrm((n_pool, PAGE_SIZE, FOX_HEADS, FOX_HEAD_DIM), 1.0)
    cache_fox_v = nrm((n_pool, PAGE_SIZE, FOX_HEADS, FOX_HEAD_DIM), 1.0)
    cache_fox_logf = jax.nn.log_sigmoid(2.0 + nrm((n_pool, PAGE_SIZE, FOX_HEADS), 1.0))
    page_table = jax.random.permutation(keys[next(ks)], n_pool)[:n_used].reshape(
        DEC_BATCH, n_pages).astype(jnp.int32)
    state_rwkv = nrm((DEC_BATCH, RWKV_HEADS, RWKV_HEAD_DIM, RWKV_HEAD_DIM), 0.3)
    state_rwkv_shift = nrm((DEC_BATCH, RWKV_COLS), 1.0)
    cache_mem_k = nrm((DEC_BATCH, MEM_TOKENS, MEM_HEADS, MEM_HEAD_DIM), 1.0)
    cache_mem_v = nrm((DEC_BATCH, MEM_TOKENS, MEM_HEADS, MEM_HEAD_DIM), 1.0)
    return {
        'x_prompt': x_prompt,
        'x_sample': x_sample,
        'mem_prompt': mem_prompt,
        'cache_fox_k': cache_fox_k,
        'cache_fox_v': cache_fox_v,
        'cache_fox_logf': cache_fox_logf,
        'page_table': page_table,
        'state_rwkv': state_rwkv,
        'state_rwkv_shift': state_rwkv_shift,
        'cache_mem_k': cache_mem_k,
        'cache_mem_v': cache_mem_v,
        'norm_mix': gain(D_MODEL),
        'w_in': nrm((D_MODEL, FOX_COLS + RWKV_COLS), D_MODEL ** -0.5),
        'fox_q_gain': gain(FOX_HEAD_DIM),
        'fox_k_gain': gain(FOX_HEAD_DIM),
        'fox_b_f': 2.0 + nrm((FOX_HEADS,), 0.1),
        'rwkv_mu': jax.random.uniform(keys[next(ks)], (RWKV_COLS,), f32),
        'rwkv_w0': -2.5 + nrm((RWKV_WIDTH,), 1.0),
        'rwkv_w2': nrm((DECAY_LORA, RWKV_WIDTH), 0.1 * DECAY_LORA ** -0.5),
        'rwkv_a0': nrm((RWKV_WIDTH,), 0.1),
        'rwkv_a2': nrm((AAA_LORA, RWKV_WIDTH), 0.1 * AAA_LORA ** -0.5),
        'rwkv_g2': nrm((GATE_LORA, RWKV_WIDTH), GATE_LORA ** -0.5),
        'rwkv_k_k': 0.85 + nrm((RWKV_WIDTH,), 0.02),
        'rwkv_k_a': gain(RWKV_WIDTH),
        'rwkv_r_k': nrm((RWKV_HEADS, RWKV_HEAD_DIM), 0.1),
        'rwkv_gn_w': gain(RWKV_WIDTH),
        'rwkv_gn_b': nrm((RWKV_WIDTH,), 0.02),
        'w_out': nrm((MIX_WIDTH, D_MODEL), MIX_WIDTH ** -0.5),
        'norm_mem_q': gain(D_MODEL),
        'norm_mem_src': gain(D_MODEL),
        'w_mem_q': nrm((D_MODEL, MEM_WIDTH), D_MODEL ** -0.5),
        'w_mem_kv': nrm((D_MODEL, 2 * MEM_WIDTH), D_MODEL ** -0.5),
        'mem_q_gain': gain(MEM_HEAD_DIM),
        'mem_k_gain': gain(MEM_HEAD_DIM),
        'w_mem_o': nrm((MEM_WIDTH, D_MODEL), MEM_WIDTH ** -0.5),
        'norm_ffn': gain(D_MODEL),
        'w_router_group': nrm((D_MODEL, N_GROUPS), D_MODEL ** -0.5),
        'w_router_expert': nrm((D_MODEL, N_EXPERTS), D_MODEL ** -0.5),
        'w_exp_gate': nrm((N_EXPERTS, D_MODEL, D_EXPERT), D_MODEL ** -0.5),
        'w_exp_up': nrm((N_EXPERTS, D_MODEL, D_EXPERT), D_MODEL ** -0.5),
        'w_exp_down': nrm((N_EXPERTS, D_EXPERT, D_MODEL), D_EXPERT ** -0.5),
    }


def reference(x_prompt, x_sample, mem_prompt, cache_fox_k, cache_fox_v, cache_fox_logf,
              page_table, state_rwkv, state_rwkv_shift, cache_mem_k, cache_mem_v,
              norm_mix, w_in, fox_q_gain, fox_k_gain, fox_b_f, rwkv_mu, rwkv_w0, rwkv_w2,
              rwkv_a0, rwkv_a2, rwkv_g2, rwkv_k_k, rwkv_k_a, rwkv_r_k, rwkv_gn_w, rwkv_gn_b,
              w_out, norm_mem_q, norm_mem_src, w_mem_q, w_mem_kv, mem_q_gain, mem_k_gain,
              w_mem_o, norm_ffn, w_router_group, w_router_expert, w_exp_gate, w_exp_up,
              w_exp_down):
    weights = (norm_mix, w_in, fox_q_gain, fox_k_gain, fox_b_f, rwkv_mu, rwkv_w0, rwkv_w2,
               rwkv_a0, rwkv_a2, rwkv_g2, rwkv_k_k, rwkv_k_a, rwkv_r_k, rwkv_gn_w, rwkv_gn_b,
               w_out, norm_mem_q, w_mem_q, mem_q_gain, w_mem_o, norm_ffn, w_router_group,
               w_router_expert, w_exp_gate, w_exp_up, w_exp_down)

    nb = x_prompt.shape[0]
    mem_k_p, mem_v_p = memory_kv(mem_prompt, norm_mem_src, w_mem_kv, mem_k_gain)
    s0 = jnp.zeros((nb, RWKV_HEADS, RWKV_HEAD_DIM, RWKV_HEAD_DIM), jnp.float32)
    shift0 = jnp.zeros((nb, RWKV_COLS), x_prompt.dtype)
    h_p = x_prompt
    for _ in range(DEPTH):
        h_p, fox_k_p, fox_v_p, fox_logf_p, rwkv_state_p, rwkv_shift_p = decoder_layer(
            h_p, mem_k_p, mem_v_p, None, s0, shift0, weights)
    y_prompt = h_p

    fox_past = (gather_pages(cache_fox_k, page_table), gather_pages(cache_fox_v, page_table),
                gather_pages(cache_fox_logf, page_table))
    h_s = x_sample
    for _ in range(DEPTH):
        h_s, fox_k_s, fox_v_s, fox_logf_s, rwkv_state_s, rwkv_shift_s = decoder_layer(
            h_s, cache_mem_k, cache_mem_v, fox_past, state_rwkv, state_rwkv_shift, weights)
    y_sample = h_s

    return (y_prompt, y_sample, fox_k_p, fox_v_p, fox_logf_p, rwkv_state_p, rwkv_shift_p,
            mem_k_p, mem_v_p, fox_k_s, fox_v_s, fox_logf_s, rwkv_state_s, rwkv_shift_s)
```

```python
import functools

import numpy as np
import jax
import jax.numpy as jnp
from jax import lax
from jax.experimental import pallas as pl
from jax.experimental.pallas import tpu as pltpu

F32 = jnp.float32
BF16 = jnp.bfloat16
HI = lax.Precision.HIGHEST

LANE = 128
FOX_HEADS = 16
FOX_HEAD_DIM = 128
FOX_WIDTH = FOX_HEADS * FOX_HEAD_DIM
RWKV_HEADS = 32
RWKV_HEAD_DIM = 64
RWKV_WIDTH = RWKV_HEADS * RWKV_HEAD_DIM
MEM_HEADS = 4
MEM_HEAD_DIM = 128
MEM_WIDTH = MEM_HEADS * MEM_HEAD_DIM
N_GROUPS = 8
EXPERTS_PER_GROUP = 8
N_EXPERTS = N_GROUPS * EXPERTS_PER_GROUP
TOP_K = 2
EXPERT_BLOCK = 128
NORM_EPS = 1e-6
RWKV_GN_EPS = 64e-5
FOX_SCALE = FOX_HEAD_DIM ** -0.5
MEM_SCALE = MEM_HEAD_DIM ** -0.5
VMEM_LIMIT = 56 * 1024 * 1024

COL_Q = 0
COL_K = FOX_WIDTH
COL_V = 2 * FOX_WIDTH
COL_RWKV = 3 * FOX_WIDTH
LORA_SLOTS = 2 * LANE + 256
RWKV_COLS_PADDED = 3 * RWKV_WIDTH + LORA_SLOTS
COL_F = COL_RWKV + RWKV_COLS_PADDED
PROJ_COLS = COL_F + LANE
PROJ_TN = 512
PROJ_COLS_PADDED = -(-PROJ_COLS // PROJ_TN) * PROJ_TN


def _cparams(sem):
    return pltpu.CompilerParams(dimension_semantics=sem, vmem_limit_bytes=VMEM_LIMIT)


def _dot(a, b, precision=None):
    return jnp.dot(a, b, preferred_element_type=F32, precision=precision)


def _dot_nt(a, b, precision=None):
    return lax.dot_general(a, b, (((1,), (1,)), ((), ())), preferred_element_type=F32,
                           precision=precision)


def _iota(shape, dim):
    return lax.broadcasted_iota(jnp.int32, shape, dim)


def _mm_body(*refs, n_lhs, has_norm, has_res, n_head_tiles, emit_lhs, exact, tn):
    it = iter(refs)
    lhs = [next(it) for _ in range(n_lhs)]
    gain = next(it) if has_norm else None
    ws = [next(it) for _ in range(n_lhs)]
    res = next(it) if has_res else None
    hgain = next(it) if n_head_tiles else None
    out = next(it)
    lhs_out = next(it) if emit_lhs else None
    xs = [next(it) for _ in range(n_lhs)]
    j = pl.program_id(1)

    @pl.when(j == 0)
    def _():
        for l_ref, x_ref in zip(lhs, xs):
            v = l_ref[...]
            if has_norm:
                v = v * lax.rsqrt(jnp.mean(v * v, axis=-1, keepdims=True) + NORM_EPS) * gain[...]
            x_ref[...] = v.astype(x_ref.dtype)
            if emit_lhs:
                lhs_out[...] = v

    acc = None
    for x_ref, w_ref in zip(xs, ws):
        d = _dot(x_ref[...], w_ref[...], HI if exact else None)
        acc = d if acc is None else acc + d
    if has_res:
        acc = acc + res[...]
    if n_head_tiles:
        @pl.when(j < n_head_tiles)
        def _():
            g = hgain[...]
            for s in range(tn // LANE):
                seg = acc[:, s * LANE:(s + 1) * LANE]
                ms = jnp.mean(seg * seg, axis=-1, keepdims=True)
                out[:, s * LANE:(s + 1) * LANE] = (seg * lax.rsqrt(ms + NORM_EPS)
                                                   * g[:, s * LANE:(s + 1) * LANE])

        @pl.when(j >= n_head_tiles)
        def _():
            out[...] = acc
    else:
        out[...] = acc


def fused_matmul(lhs, ws, *, norm_gain=None, residual=None, head_gain=None, n_head_tiles=0,
                 emit_lhs=False, exact=False, tm=512, tn=512, name="fused_matmul"):
    n_lhs = len(lhs)
    m = lhs[0].shape[0]
    n = ws[0].shape[1]
    tm = min(tm, m)
    tn = min(tn, n)
    assert m % tm == 0 and n % tn == 0
    has_norm = norm_gain is not None
    has_res = residual is not None
    assert not (has_norm and n_lhs != 1)
    in_specs, args = [], []
    for l in lhs:
        in_specs.append(pl.BlockSpec((tm, l.shape[1]), lambda i, j: (i, 0)))
        args.append(l)
    if has_norm:
        in_specs.append(pl.BlockSpec((1, lhs[0].shape[1]), lambda i, j: (0, 0)))
        args.append(norm_gain.reshape(1, -1).astype(F32))
    for w in ws:
        in_specs.append(pl.BlockSpec((w.shape[0], tn), lambda i, j: (0, j)))
        args.append(w)
    if has_res:
        in_specs.append(pl.BlockSpec((tm, tn), lambda i, j: (i, j)))
        args.append(residual)
    if n_head_tiles:
        in_specs.append(pl.BlockSpec((1, tn), lambda i, j: (0, j)))
        args.append(head_gain.reshape(1, -1).astype(F32))
    out_shape = [jax.ShapeDtypeStruct((m, n), F32)]
    out_specs = [pl.BlockSpec((tm, tn), lambda i, j: (i, j))]
    if emit_lhs:
        out_shape.append(jax.ShapeDtypeStruct((m, lhs[0].shape[1]), F32))
        out_specs.append(pl.BlockSpec((tm, lhs[0].shape[1]), lambda i, j: (i, 0)))
    scratch = [pltpu.VMEM((tm, l.shape[1]), F32 if exact else BF16) for l in lhs]
    body = functools.partial(_mm_body, n_lhs=n_lhs, has_norm=has_norm, has_res=has_res,
                             n_head_tiles=n_head_tiles, emit_lhs=emit_lhs, exact=exact, tn=tn)
    res = pl.pallas_call(
        body, grid=(m // tm, n // tn), in_specs=in_specs, out_specs=out_specs,
        out_shape=out_shape, scratch_shapes=scratch,
        compiler_params=_cparams(("parallel", "arbitrary")), name=name)(*args)
    return res if emit_lhs else res[0]


def _log_sigmoid(x):
    return jnp.minimum(x, 0.0) - jnp.log1p(jnp.exp(-jnp.abs(x)))


def _logf_body(f_ref, b_ref, logf_ref, c_ref, carry, *, tc):
    i = pl.program_id(0)

    @pl.when(i == 0)
    def _():
        carry[...] = jnp.zeros_like(carry)

    lf = _log_sigmoid(f_ref[:, :FOX_HEADS] + b_ref[...])
    logf_ref[...] = lf
    tri = (_iota((tc, tc), 0) >= _iota((tc, tc), 1)).astype(F32)
    c = _dot(tri, lf, HI) + carry[...]
    c_ref[...] = c
    carry[...] = c[tc - 1:tc, :]


def fox_logf_cumsum(proj, row0, rows, b_f, *, tc=512):
    tc = min(tc, rows)
    assert rows % tc == 0 and row0 % tc == 0
    rb0 = row0 // tc
    return pl.pallas_call(
        functools.partial(_logf_body, tc=tc), grid=(rows // tc,),
        in_specs=[pl.BlockSpec((tc, LANE), lambda i: (rb0 + i, COL_F // LANE)),
                  pl.BlockSpec((1, FOX_HEADS), lambda i: (0, 0))],
        out_specs=[pl.BlockSpec((tc, FOX_HEADS), lambda i: (i, 0)),
                   pl.BlockSpec((tc, FOX_HEADS), lambda i: (i, 0))],
        out_shape=[jax.ShapeDtypeStruct((rows, FOX_HEADS), F32)] * 2,
        scratch_shapes=[pltpu.VMEM((1, FOX_HEADS), F32)],
        compiler_params=_cparams(("arbitrary",)), name="fox_logf_cumsum")(
            proj, b_f.reshape(1, FOX_HEADS).astype(F32))


def _flash_body(qi_tbl, ki_tbl, q_ref, k_ref, v_ref, c_ref, ct_ref, o_ref, m_s, l_s, acc_s, cq_s,
                *, tq, tk):
    h = pl.program_id(0)
    s_idx = pl.program_id(1)
    qi = qi_tbl[s_idx]
    ki = ki_tbl[s_idx]

    @pl.when(ki == 0)
    def _():
        m_s[...] = jnp.full_like(m_s, -jnp.inf)
        l_s[...] = jnp.zeros_like(l_s)
        acc_s[...] = jnp.zeros_like(acc_s)
        lane = _iota((tq, FOX_HEADS), 1)
        cq_s[...] = jnp.sum(jnp.where(lane == h, c_ref[...], 0.0), axis=1, keepdims=True)

    s = _dot_nt(q_ref[...].astype(BF16), k_ref[...].astype(BF16)) * FOX_SCALE
    s = s + cq_s[...] - ct_ref[...]
    q_pos = qi * tq + _iota((tq, tk), 0)
    k_pos = ki * tk + _iota((tq, tk), 1)
    s = jnp.where(k_pos <= q_pos, s, -jnp.inf)
    m_prev = m_s[...]
    m_new = jnp.maximum(m_prev, jnp.max(s, axis=1, keepdims=True))
    alpha = jnp.exp(m_prev - m_new)
    p = jnp.exp(s - m_new)
    l_s[...] = alpha * l_s[...] + jnp.sum(p, axis=1, keepdims=True)
    acc_s[...] = alpha * acc_s[...] + _dot(p.astype(BF16), v_ref[...].astype(BF16))
    m_s[...] = m_new

    @pl.when(ki == (qi + 1) * (tq // tk) - 1)
    def _():
        o_ref[...] = acc_s[...] / l_s[...]


def fox_prompt_attention(proj, c, seq, *, tq=1024, tk=1024):
    tq = min(tq, seq)
    tk = min(tk, tq)
    assert seq % tq == 0 and tq % tk == 0
    nq = seq // tq
    qi_np, ki_np = [], []
    for qi in range(nq):
        for ki in range((qi + 1) * (tq // tk)):
            qi_np.append(qi)
            ki_np.append(ki)
    qi_tbl = jnp.asarray(np.array(qi_np, np.int32))
    ki_tbl = jnp.asarray(np.array(ki_np, np.int32))
    ct = c.T.reshape(FOX_HEADS, 1, seq)
    grid_spec = pltpu.PrefetchScalarGridSpec(
        num_scalar_prefetch=2, grid=(FOX_HEADS, len(qi_np)),
        in_specs=[
            pl.BlockSpec((tq, FOX_HEAD_DIM), lambda h, s, qt, kt: (qt[s], COL_Q // LANE + h)),
            pl.BlockSpec((tk, FOX_HEAD_DIM), lambda h, s, qt, kt: (kt[s], COL_K // LANE + h)),
            pl.BlockSpec((tk, FOX_HEAD_DIM), lambda h, s, qt, kt: (kt[s], COL_V // LANE + h)),
            pl.BlockSpec((tq, FOX_HEADS), lambda h, s, qt, kt: (qt[s], 0)),
            pl.BlockSpec((None, 1, tk), lambda h, s, qt, kt: (h, 0, kt[s])),
        ],
        out_specs=pl.BlockSpec((tq, FOX_HEAD_DIM), lambda h, s, qt, kt: (qt[s], h)),
        scratch_shapes=[pltpu.VMEM((tq, 1), F32), pltpu.VMEM((tq, 1), F32),
                        pltpu.VMEM((tq, FOX_HEAD_DIM), F32), pltpu.VMEM((tq, 1), F32)])
    return pl.pallas_call(
        functools.partial(_flash_body, tq=tq, tk=tk), grid_spec=grid_spec,
        out_shape=jax.ShapeDtypeStruct((seq, FOX_WIDTH), F32),
        compiler_params=_cparams(("parallel", "arbitrary")), name="fox_prompt_attention")(
            qi_tbl, ki_tbl, proj, proj, proj, c, ct)


def _diag_blocks(full, rows_per_head):
    return jnp.concatenate(
        [full[h * rows_per_head:(h + 1) * rows_per_head, h * FOX_HEAD_DIM:(h + 1) * FOX_HEAD_DIM]
         for h in range(FOX_HEADS)], axis=0)


def _decode_body(pt_ref, q_ref, kn_ref, vn_ref, lfn_ref, kc_ref, vc_ref, lfc_ref, o_ref,
                 qbd_s, m_s, l_s, acc_s, carry_s, ccol_s, *, n_pages, tnew, page):
    p = pl.program_id(1)
    rows = FOX_HEADS * tnew
    head_sel = (_iota((rows, FOX_HEADS), 0) // tnew == _iota((rows, FOX_HEADS), 1)).astype(F32)

    @pl.when(p == 0)
    def _():
        q = q_ref[...]
        qt = jnp.concatenate([q] * FOX_HEADS, axis=0)
        rh = _iota((rows, FOX_WIDTH), 0) // tnew
        ch = _iota((rows, FOX_WIDTH), 1) // FOX_HEAD_DIM
        qbd = jnp.where(rh == ch, qt, 0.0).astype(BF16)
        qbd_s[...] = qbd
        pad = LANE - tnew
        lfn = jnp.concatenate([lfn_ref[...], jnp.zeros((pad, FOX_HEADS), F32)], axis=0)
        a2 = _dot_nt(head_sel, lfn, HI)
        incl = (_iota((LANE, LANE), 0) <= _iota((LANE, LANE), 1)).astype(F32)
        crow = _dot(a2, incl, HI)
        lane = _iota((rows, LANE), 1)
        rq = _iota((rows, LANE), 0) % tnew
        ccol = jnp.sum(jnp.where(lane == rq, crow, 0.0), axis=1, keepdims=True)
        ccol_s[...] = ccol
        kn = jnp.concatenate([kn_ref[...], jnp.zeros((pad, FOX_WIDTH), F32)], axis=0).astype(BF16)
        vn = jnp.concatenate([vn_ref[...], jnp.zeros((pad, FOX_WIDTH), F32)], axis=0).astype(BF16)
        s = _dot_nt(qbd, kn) * FOX_SCALE + ccol - crow
        s = jnp.where(lane <= rq, s, -jnp.inf)
        m = jnp.max(s, axis=1, keepdims=True)
        pr = jnp.exp(s - m)
        m_s[...] = m
        l_s[...] = jnp.sum(pr, axis=1, keepdims=True)
        acc_s[...] = _diag_blocks(_dot(pr.astype(BF16), vn), tnew)
        carry_s[...] = jnp.zeros_like(carry_s)

    s = _dot_nt(qbd_s[...], kc_ref[...].astype(BF16)) * FOX_SCALE
    a = _dot_nt(head_sel, lfc_ref[...], HI)
    later = (_iota((page, page), 0) > _iota((page, page), 1)).astype(F32)
    sfx = _dot(a, later, HI) + carry_s[...]
    s = s + ccol_s[...] + sfx
    m_prev = m_s[...]
    m_new = jnp.maximum(m_prev, jnp.max(s, axis=1, keepdims=True))
    alpha = jnp.exp(m_prev - m_new)
    pr = jnp.exp(s - m_new)
    l_s[...] = alpha * l_s[...] + jnp.sum(pr, axis=1, keepdims=True)
    acc_s[...] = alpha * acc_s[...] + _diag_blocks(
        _dot(pr.astype(BF16), vc_ref[...].astype(BF16)), tnew)
    m_s[...] = m_new
    carry_s[...] = carry_s[...] + jnp.sum(a, axis=1, keepdims=True)

    @pl.when(p == n_pages - 1)
    def _():
        o = acc_s[...] / l_s[...]
        o_ref[...] = jnp.concatenate([o[h * tnew:(h + 1) * tnew, :] for h in range(FOX_HEADS)],
                                     axis=1)


def fox_decode_attention(proj, row0, logf_new, cache_k, cache_v, cache_logf, page_table, tnew):
    n_seq, n_pages = page_table.shape
    n_pool, page = cache_k.shape[0], cache_k.shape[1]
    assert row0 % tnew == 0 and tnew == 8
    rb0 = row0 // tnew
    kc = cache_k.reshape(n_pool, page, FOX_WIDTH)
    vc = cache_v.reshape(n_pool, page, FOX_WIDTH)
    rows = FOX_HEADS * tnew
    grid_spec = pltpu.PrefetchScalarGridSpec(
        num_scalar_prefetch=1, grid=(n_seq, n_pages),
        in_specs=[
            pl.BlockSpec((tnew, FOX_WIDTH), lambda b, p, pt: (rb0 + b, COL_Q // FOX_WIDTH)),
            pl.BlockSpec((tnew, FOX_WIDTH), lambda b, p, pt: (rb0 + b, COL_K // FOX_WIDTH)),
            pl.BlockSpec((tnew, FOX_WIDTH), lambda b, p, pt: (rb0 + b, COL_V // FOX_WIDTH)),
            pl.BlockSpec((tnew, FOX_HEADS), lambda b, p, pt: (b, 0)),
            pl.BlockSpec((None, page, FOX_WIDTH), lambda b, p, pt: (pt[b, n_pages - 1 - p], 0, 0)),
            pl.BlockSpec((None, page, FOX_WIDTH), lambda b, p, pt: (pt[b, n_pages - 1 - p], 0, 0)),
            pl.BlockSpec((None, page, FOX_HEADS), lambda b, p, pt: (pt[b, n_pages - 1 - p], 0, 0)),
        ],
        out_specs=pl.BlockSpec((tnew, FOX_WIDTH), lambda b, p, pt: (b, 0)),
        scratch_shapes=[pltpu.VMEM((rows, FOX_WIDTH), BF16), pltpu.VMEM((rows, 1), F32),
                        pltpu.VMEM((rows, 1), F32), pltpu.VMEM((rows, FOX_HEAD_DIM), F32),
                        pltpu.VMEM((rows, 1), F32), pltpu.VMEM((rows, 1), F32)])
    return pl.pallas_call(
        functools.partial(_decode_body, n_pages=n_pages, tnew=tnew, page=page),
        grid_spec=grid_spec, out_shape=jax.ShapeDtypeStruct((n_seq * tnew, FOX_WIDTH), F32),
        compiler_params=_cparams(("parallel", "arbitrary")), name="fox_decode_attention")(
            page_table, proj, proj, proj, logf_new, kc, vc, cache_logf)


def _head_sum_matrix(width):
    return (_iota((width, width), 0) // RWKV_HEAD_DIM
            == _iota((width, width), 1) // RWKV_HEAD_DIM)


def _head_sum_exact(x):
    pm = _head_sum_matrix(LANE).astype(F32)
    return jnp.concatenate([_dot(x[:, j * LANE:(j + 1) * LANE], pm, HI)
                            for j in range(x.shape[1] // LANE)], axis=1)


def _softplus(y):
    return jnp.maximum(y, 0.0) + jnp.log1p(jnp.exp(-jnp.abs(y)))


def _rwkv_prep_body(pr_ref, pk_ref, pv_ref, pl_ref, qr_ref, qk_ref, qv_ref, ql_ref,
                    mr_ref, mk_ref, mv_ref, ml_ref, w0_ref, w2_ref, a0_ref, a2_ref, g2_ref,
                    kk_ref, ka_ref, r_o, w_o, k_o, v_o, a_o, b_o, g_o):
    def shift(p_ref, q_ref, m_ref):
        p = p_ref[...]
        return p + m_ref[...] * (q_ref[...] - p)

    r = shift(pr_ref, qr_ref, mr_ref)
    k = shift(pk_ref, qk_ref, mk_ref)
    v = shift(pv_ref, qv_ref, mv_ref)
    zl = shift(pl_ref, ql_ref, ml_ref)
    xw = zl[:, 0:LANE]
    xa = zl[:, LANE:2 * LANE]
    xg = zl[:, 2 * LANE:]
    lw = w0_ref[...] + _dot(jnp.tanh(xw).astype(BF16), w2_ref[...])
    w_log = -_softplus(-lw) - 0.5
    a = jax.nn.sigmoid(a0_ref[...] + _dot(xa.astype(BF16), a2_ref[...]))
    kk = k * kk_ref[...]
    kk = kk * lax.rsqrt(jnp.maximum(_head_sum_exact(kk * kk), 1e-24))
    r_o[...] = r
    w_o[...] = jnp.exp(-jnp.exp(w_log))
    k_o[...] = k * (1.0 + (a - 1.0) * ka_ref[...])
    v_o[...] = v
    a_o[...] = -kk
    b_o[...] = kk * a
    g_o[...] = _dot(jax.nn.sigmoid(xg).astype(BF16), g2_ref[...])


def rwkv_prep(proj, prev, mu, w0, w2, a0, a2, g2, k_k, k_a, *, tm=128):
    m = proj.shape[0]
    tm = min(tm, m)
    W = RWKV_WIDTH
    assert m % tm == 0 and COL_RWKV % W == 0 and (COL_RWKV + 3 * W) % LORA_SLOTS == 0
    row = lambda a: a.reshape(1, -1).astype(F32)
    vec = pl.BlockSpec((1, W), lambda i: (0, 0))

    def pieces(rows, col0):
        specs = [pl.BlockSpec((rows, W), functools.partial(lambda i, c: (i if rows > 1 else 0, c),
                                                           c=col0 // W + n)) for n in range(3)]
        specs.append(pl.BlockSpec((rows, LORA_SLOTS),
                                  lambda i: (i if rows > 1 else 0, (col0 + 3 * W) // LORA_SLOTS)))
        return specs

    return pl.pallas_call(
        _rwkv_prep_body, grid=(m // tm,),
        in_specs=pieces(tm, COL_RWKV) + pieces(tm, 0) + pieces(1, 0) + [
            vec, pl.BlockSpec((LANE, W), lambda i: (0, 0)), vec,
            pl.BlockSpec((LANE, W), lambda i: (0, 0)),
            pl.BlockSpec((256, W), lambda i: (0, 0)), vec, vec],
        out_specs=[pl.BlockSpec((tm, W), lambda i: (i, 0))] * 7,
        out_shape=[jax.ShapeDtypeStruct((m, W), F32)] * 7,
        compiler_params=_cparams(("parallel",)), name="rwkv_prep")(
            *([proj] * 4 + [prev] * 4 + [mu] * 4),
            row(w0), w2, row(a0), a2, g2, row(k_k), row(k_a))


def _split_bf16(x):
    hi = x.astype(BF16)
    lo = (x - hi.astype(F32)).astype(BF16)
    return hi, lo


def _scan_body(r_ref, w_ref, k_ref, v_ref, a_ref, b_ref, s0_ref, o_ref, st_ref, state, *, tc,
               n_chunks, unroll):
    c = pl.program_id(1)
    N = RWKV_HEAD_DIM
    W = RWKV_WIDTH
    SEG = 256

    @pl.when(c == 0)
    def _():
        state[...] = s0_ref[...]

    diag = (_iota((N, W), 1) % N == _iota((N, W), 0)).astype(F32)
    pm = _head_sum_matrix(SEG).astype(BF16)

    def head_sum(x, two_pass):
        hi, lo = _split_bf16(x) if two_pass else (x.astype(BF16), None)
        outs = []
        for j in range(W // SEG):
            d = _dot(hi[:, j * SEG:(j + 1) * SEG], pm)
            if two_pass:
                d = d + _dot(lo[:, j * SEG:(j + 1) * SEG], pm)
            outs.append(d)
        return jnp.concatenate(outs, axis=1)

    def step(t, carry):
        r = r_ref[pl.ds(t, 1), :]
        w = w_ref[pl.ds(t, 1), :]
        k = k_ref[pl.ds(t, 1), :]
        v = v_ref[pl.ds(t, 1), :]
        a = a_ref[pl.ds(t, 1), :]
        b = b_ref[pl.ds(t, 1), :]
        s = state[...]
        sa = head_sum(s * a, True)
        vc = head_sum(v * diag, True)
        s_new = s * w + sa * b + vc * k
        state[...] = s_new
        o_bc = head_sum(s_new * r, True)
        o_ref[pl.ds(t, 1), :] = jnp.sum(o_bc * diag, axis=0, keepdims=True)
        return carry

    lax.fori_loop(0, tc, step, 0, unroll=unroll)

    @pl.when(c == n_chunks - 1)
    def _():
        st_ref[...] = state[...]


def rwkv_scan(seqs, row0, n_seq, seq_len, s0, *, tc=256, unroll=2):
    tc = min(tc, seq_len)
    assert seq_len % tc == 0 and row0 % tc == 0
    n_chunks = seq_len // tc
    rb0 = row0 // tc
    W = RWKV_WIDTH
    tok = pl.BlockSpec((tc, W), lambda s, c: (rb0 + s * n_chunks + c, 0))
    st = pl.BlockSpec((None, RWKV_HEAD_DIM, W), lambda s, c: (s, 0, 0))
    return pl.pallas_call(
        functools.partial(_scan_body, tc=tc, n_chunks=n_chunks, unroll=min(unroll, tc)),
        grid=(n_seq, n_chunks),
        in_specs=[tok] * 6 + [st],
        out_specs=[pl.BlockSpec((tc, W), lambda s, c: (s * n_chunks + c, 0)), st],
        out_shape=[jax.ShapeDtypeStruct((n_seq * seq_len, W), F32),
                   jax.ShapeDtypeStruct((n_seq, RWKV_HEAD_DIM, W), F32)],
        scratch_shapes=[pltpu.VMEM((RWKV_HEAD_DIM, W), F32)],
        compiler_params=_cparams(("parallel", "arbitrary")), name="rwkv_scan")(*seqs, s0)


def _rwkv_post_body(o_ref, r_ref, k_ref, v_ref, g_ref, rk_ref, gw_ref, gb_ref, out_ref):
    o = o_ref[...]
    inv_n = 1.0 / RWKV_HEAD_DIM
    mean = _head_sum_exact(o) * inv_n
    d = o - mean
    var = _head_sum_exact(d * d) * inv_n
    y = d * lax.rsqrt(var + RWKV_GN_EPS) * gw_ref[...] + gb_ref[...]
    y = y + _head_sum_exact(r_ref[...] * k_ref[...] * rk_ref[...]) * v_ref[...]
    out_ref[...] = y * g_ref[...]


def rwkv_post(o, r, k, v, g, r_k, gn_w, gn_b, *, tm=256):
    m = o.shape[0]
    tm = min(tm, m)
    assert m % tm == 0
    W = RWKV_WIDTH
    row = lambda a: a.reshape(1, -1).astype(F32)
    tok = pl.BlockSpec((tm, W), lambda i: (i, 0))
    vec = pl.BlockSpec((1, W), lambda i: (0, 0))
    return pl.pallas_call(
        _rwkv_post_body, grid=(m // tm,), in_specs=[tok] * 5 + [vec] * 3, out_specs=tok,
        out_shape=jax.ShapeDtypeStruct((m, W), F32),
        compiler_params=_cparams(("parallel",)), name="rwkv_post")(
            o, r, k, v, g, row(r_k), row(gn_w), row(gn_b))


def _memattn_body(q_ref, k_ref, v_ref, o_ref):
    q = q_ref[...]
    k = k_ref[...]
    v = v_ref[...]
    outs = []
    for h in range(MEM_HEADS):
        sl = slice(h * MEM_HEAD_DIM, (h + 1) * MEM_HEAD_DIM)
        s = _dot_nt(q[:, sl].astype(BF16), k[:, sl].astype(BF16)) * MEM_SCALE
        e = jnp.exp(s - jnp.max(s, axis=1, keepdims=True))
        pr = e / jnp.sum(e, axis=1, keepdims=True)
        outs.append(_dot(pr.astype(BF16), v[:, sl].astype(BF16)))
    o_ref[...] = jnp.concatenate(outs, axis=1)


def memory_attention(q, row0, n_seq, seq_len, mem_k, mem_v, *, tq=512):
    tq = min(tq, seq_len)
    assert seq_len % tq == 0 and row0 % tq == 0
    nt = seq_len // tq
    rb0 = row0 // tq
    m_tok = mem_k.shape[1]
    kv = pl.BlockSpec((None, m_tok, MEM_WIDTH), lambda b, t: (b, 0, 0))
    return pl.pallas_call(
        _memattn_body, grid=(n_seq, nt),
        in_specs=[pl.BlockSpec((tq, MEM_WIDTH), lambda b, t: (rb0 + b * nt + t, 0)), kv, kv],
        out_specs=pl.BlockSpec((tq, MEM_WIDTH), lambda b, t: (b * nt + t, 0)),
        out_shape=jax.ShapeDtypeStruct((n_seq * seq_len, MEM_WIDTH), F32),
        compiler_params=_cparams(("parallel", "parallel")), name="memory_attention")(
            q, mem_k, mem_v)


def _router_body(lg_ref, idx_ref, wt_ref):
    lg = lg_ref[...]
    shape = lg.shape
    lane = _iota(shape, 1).astype(F32)
    neg = -jnp.inf
    big = float(LANE)
    is_g = lane < N_GROUPS
    gl = jnp.where(is_g, lg, neg)
    gmax = jnp.max(gl, axis=1, keepdims=True)
    gsel = jnp.min(jnp.where(gl == gmax, lane, big), axis=1, keepdims=True)
    gsum = jnp.sum(jnp.where(is_g, jnp.exp(lg - gmax), 0.0), axis=1, keepdims=True)
    g_w = 1.0 / gsum
    base = N_GROUPS + gsel * EXPERTS_PER_GROUP
    in_g = jnp.logical_and(lane >= base, lane < base + EXPERTS_PER_GROUP)
    el = jnp.where(in_g, lg, neg)
    emax = jnp.max(el, axis=1, keepdims=True)
    ee = jnp.where(in_g, jnp.exp(lg - emax), 0.0)
    prob = ee / jnp.sum(ee, axis=1, keepdims=True)
    pm = jnp.where(in_g, prob, -1.0)
    p1 = jnp.max(pm, axis=1, keepdims=True)
    i1 = jnp.min(jnp.where(pm == p1, lane, big), axis=1, keepdims=True)
    pm2 = jnp.where(lane == i1, -1.0, pm)
    p2 = jnp.max(pm2, axis=1, keepdims=True)
    i2 = jnp.min(jnp.where(pm2 == p2, lane, big), axis=1, keepdims=True)
    tot = p1 + p2
    idx_ref[...] = jnp.where(lane == 0, i1 - N_GROUPS,
                             jnp.where(lane == 1, i2 - N_GROUPS, 0.0)).astype(jnp.int32)
    wt_ref[...] = jnp.where(lane == 0, g_w * (p1 / tot), jnp.where(lane == 1, g_w * (p2 / tot), 0.0))


def moe_router(logits, *, tm=512):
    m = logits.shape[0]
    tm = min(tm, m)
    assert m % tm == 0
    blk = pl.BlockSpec((tm, LANE), lambda i: (i, 0))
    return pl.pallas_call(
        _router_body, grid=(m // tm,), in_specs=[blk], out_specs=[blk, blk],
        out_shape=[jax.ShapeDtypeStruct((m, LANE), jnp.int32), jax.ShapeDtypeStruct((m, LANE), F32)],
        compiler_params=_cparams(("parallel",)), name="moe_router")(logits)


def _row_copy(src, dst, src_row, dst_row, sem):
    return pltpu.make_async_copy(src.at[pl.ds(src_row, 1)], dst.at[pl.ds(dst_row, 1)], sem)


def _permute_body(tok_ref, nused_ref, x_hbm, o_hbm, sem):
    i = pl.program_id(0)

    @pl.when(i < nused_ref[0])
    def _():
        base = i * EXPERT_BLOCK

        def issue(r, carry):
            _row_copy(x_hbm, o_hbm, tok_ref[base + r], base + r, sem).start()
            return carry

        lax.fori_loop(0, EXPERT_BLOCK, issue, 0)
        pltpu.make_async_copy(x_hbm.at[pl.ds(0, EXPERT_BLOCK)],
                              o_hbm.at[pl.ds(base, EXPERT_BLOCK)], sem).wait()


def moe_permute(x_pad, slot_tok, n_used, n_blocks):
    d = x_pad.shape[1]
    grid_spec = pltpu.PrefetchScalarGridSpec(
        num_scalar_prefetch=2, grid=(n_blocks,),
        in_specs=[pl.BlockSpec(memory_space=pl.ANY)],
        out_specs=pl.BlockSpec(memory_space=pl.ANY),
        scratch_shapes=[pltpu.SemaphoreType.DMA(())])
    return pl.pallas_call(
        _permute_body, grid_spec=grid_spec,
        out_shape=jax.ShapeDtypeStruct((n_blocks * EXPERT_BLOCK, d), x_pad.dtype),
        compiler_params=_cparams(("arbitrary",)), name="moe_permute")(slot_tok, n_used, x_pad)


def _experts_body(be_ref, nused_ref, x_ref, wg_ref, wu_ref, wd_ref, y_ref):
    i = pl.program_id(0)

    @pl.when(i < nused_ref[0])
    def _():
        x = x_ref[...].astype(BF16)
        hb = jax.nn.silu(_dot(x, wg_ref[...])) * _dot(x, wu_ref[...])
        y_ref[...] = _dot(hb.astype(BF16), wd_ref[...])


def moe_experts(xs, wg, wu, wd, block_expert, n_used, n_blocks):
    d = xs.shape[1]
    de = wg.shape[2]
    rows = lambda i, be, nu: (jnp.minimum(i, nu[0] - 1), 0)
    grid_spec = pltpu.PrefetchScalarGridSpec(
        num_scalar_prefetch=2, grid=(n_blocks,),
        in_specs=[pl.BlockSpec((EXPERT_BLOCK, d), rows),
                  pl.BlockSpec((None, d, de), lambda i, be, nu: (be[i], 0, 0)),
                  pl.BlockSpec((None, d, de), lambda i, be, nu: (be[i], 0, 0)),
                  pl.BlockSpec((None, de, d), lambda i, be, nu: (be[i], 0, 0))],
        out_specs=pl.BlockSpec((EXPERT_BLOCK, d), rows))
    return pl.pallas_call(
        _experts_body, grid_spec=grid_spec,
        out_shape=jax.ShapeDtypeStruct((n_blocks * EXPERT_BLOCK, d), F32),
        compiler_params=_cparams(("arbitrary",)), name="moe_experts")(
            block_expert, n_used, xs, wg, wu, wd)


def _combine_body(slot_ref, yb_hbm, h_ref, w_ref, y_ref, g_s, sem, *, tm):
    i = pl.program_id(0)

    def issue(r, carry):
        tok = i * tm + r
        for kk in range(TOP_K):
            pltpu.make_async_copy(yb_hbm.at[pl.ds(slot_ref[TOP_K * tok + kk], 1)],
                                  g_s.at[kk, pl.ds(r, 1)], sem).start()
        return carry

    lax.fori_loop(0, tm, issue, 0)
    for kk in range(TOP_K):
        pltpu.make_async_copy(yb_hbm.at[pl.ds(0, tm)], g_s.at[kk], sem).wait()
    w = w_ref[...]
    y_ref[...] = h_ref[...] + (g_s[0] * w[:, 0:1] + g_s[1] * w[:, 1:2])


def moe_combine(yb, h, wts, slots, *, tm=128):
    m, d = h.shape
    tm = min(tm, m)
    assert m % tm == 0
    grid_spec = pltpu.PrefetchScalarGridSpec(
        num_scalar_prefetch=1, grid=(m // tm,),
        in_specs=[pl.BlockSpec(memory_space=pl.ANY),
                  pl.BlockSpec((tm, d), lambda i, s: (i, 0)),
                  pl.BlockSpec((tm, LANE), lambda i, s: (i, 0))],
        out_specs=pl.BlockSpec((tm, d), lambda i, s: (i, 0)),
        scratch_shapes=[pltpu.VMEM((TOP_K, tm, d), F32), pltpu.SemaphoreType.DMA(())])
    return pl.pallas_call(
        functools.partial(_combine_body, tm=tm), grid_spec=grid_spec,
        out_shape=jax.ShapeDtypeStruct((m, d), F32),
        compiler_params=_cparams(("arbitrary",)), name="moe_combine")(slots, yb, h, wts)


def moe_dispatch_tables(expert_idx, n_tok):
    n_assign = n_tok * TOP_K
    flat_e = expert_idx.reshape(n_assign).astype(jnp.int32)
    flat_tok = jnp.repeat(jnp.arange(n_tok, dtype=jnp.int32), TOP_K)
    order = jnp.argsort(flat_e)
    e_sorted = flat_e[order]
    counts = jnp.zeros((N_EXPERTS,), jnp.int32).at[flat_e].add(1)
    padded = (counts + EXPERT_BLOCK - 1) // EXPERT_BLOCK * EXPERT_BLOCK
    pad_end = jnp.cumsum(padded)
    pad_start = pad_end - padded
    start = jnp.cumsum(counts) - counts
    dest = pad_start[e_sorted] + jnp.arange(n_assign, dtype=jnp.int32) - start[e_sorted]
    n_blocks = -(-n_assign // EXPERT_BLOCK) + N_EXPERTS
    n_slots = n_blocks * EXPERT_BLOCK
    slot_tok = jnp.full((n_slots,), n_tok, jnp.int32).at[dest].set(flat_tok[order])
    block_start = jnp.arange(n_blocks, dtype=jnp.int32) * EXPERT_BLOCK
    block_expert = jnp.minimum(jnp.searchsorted(pad_end, block_start, side='right'),
                               N_EXPERTS - 1).astype(jnp.int32)
    slots = jnp.zeros((n_assign,), jnp.int32).at[order].set(dest)
    n_used = (pad_end[-1:] // EXPERT_BLOCK).astype(jnp.int32)
    return slot_tok, block_expert, slots, n_used, n_blocks


def _pad_cols(a, width):
    return jnp.pad(a, ((0, 0), (0, width - a.shape[1])))


def _pad_rows(a, height):
    return jnp.pad(a, ((0, height - a.shape[0]), (0, 0)))


def _rwkv_cols_to_slots(a, lora_w, lora_a):
    W = RWKV_WIDTH
    return jnp.concatenate([a[:, :3 * W], _pad_cols(a[:, 3 * W:3 * W + lora_w], LANE),
                            _pad_cols(a[:, 3 * W + lora_w:3 * W + lora_w + lora_a], LANE),
                            a[:, 3 * W + lora_w + lora_a:]], axis=1)


def _rwkv_slots_to_cols(a, lora_w, lora_a):
    W = RWKV_WIDTH
    return jnp.concatenate([a[:, :3 * W], a[:, 3 * W:3 * W + lora_w],
                            a[:, 3 * W + LANE:3 * W + LANE + lora_a], a[:, 3 * W + 2 * LANE:]], axis=1)


def kernel(x_prompt, x_sample, mem_prompt, cache_fox_k, cache_fox_v, cache_fox_logf, page_table, state_rwkv, state_rwkv_shift, cache_mem_k, cache_mem_v, norm_mix, w_in, fox_q_gain, fox_k_gain, fox_b_f, rwkv_mu, rwkv_w0, rwkv_w2, rwkv_a0, rwkv_a2, rwkv_g2, rwkv_k_k, rwkv_k_a, rwkv_r_k, rwkv_gn_w, rwkv_gn_b, w_out, norm_mem_q, norm_mem_src, w_mem_q, w_mem_kv, mem_q_gain, mem_k_gain, w_mem_o, norm_ffn, w_router_group, w_router_expert, w_exp_gate, w_exp_up, w_exp_down):
    n_b, seq, d_model = x_prompt.shape
    n_dec, dec_seq, _ = x_sample.shape
    assert n_b == 1
    n_p = n_b * seq
    n_s = n_dec * dec_seq
    n_tok = n_p + n_s
    lora_w = rwkv_w2.shape[0]
    lora_a = rwkv_a2.shape[0]
    W = RWKV_WIDTH

    x = jnp.concatenate([x_prompt.reshape(n_p, d_model), x_sample.reshape(n_s, d_model)], axis=0)

    fox_w, rwkv_w = w_in[:, :3 * FOX_WIDTH + FOX_HEADS], w_in[:, 3 * FOX_WIDTH + FOX_HEADS:]
    w1 = jnp.concatenate([fox_w[:, :3 * FOX_WIDTH], _rwkv_cols_to_slots(rwkv_w, lora_w, lora_a),
                          _pad_cols(fox_w[:, 3 * FOX_WIDTH:], LANE)], axis=1)
    w1 = _pad_cols(w1, PROJ_COLS_PADDED).astype(BF16)
    head_gain = jnp.concatenate([jnp.tile(fox_q_gain, FOX_HEADS), jnp.tile(fox_k_gain, FOX_HEADS),
                                 jnp.ones((PROJ_COLS_PADDED - 2 * FOX_WIDTH,), F32)])
    proj = fused_matmul([x], [w1], norm_gain=norm_mix, head_gain=head_gain,
                        n_head_tiles=2 * FOX_WIDTH // PROJ_TN, tn=PROJ_TN, name="proj_in")

    logf_p, c_p = fox_logf_cumsum(proj, 0, n_p, fox_b_f)
    logf_s, _ = fox_logf_cumsum(proj, n_p, n_s, fox_b_f)
    fox_out_p = fox_prompt_attention(proj, c_p, seq)
    fox_out_s = fox_decode_attention(proj, n_p, logf_s, cache_fox_k, cache_fox_v, cache_fox_logf,
                                     page_table, dec_seq)
    fox_out = jnp.concatenate([fox_out_p, fox_out_s], axis=0)

    p_rwkv = proj[:, COL_RWKV:COL_RWKV + RWKV_COLS_PADDED]
    shift_s = _rwkv_cols_to_slots(state_rwkv_shift, lora_w, lora_a)
    prev_p = jnp.concatenate([jnp.zeros((1, RWKV_COLS_PADDED), F32), p_rwkv[:n_p - 1]], axis=0)
    p_s = p_rwkv[n_p:].reshape(n_dec, dec_seq, RWKV_COLS_PADDED)
    prev_s = jnp.concatenate([shift_s[:, None, :], p_s[:, :-1]], axis=1).reshape(n_s, -1)
    prev = jnp.concatenate([prev_p, prev_s], axis=0)
    mu = _rwkv_cols_to_slots(rwkv_mu.reshape(1, -1), lora_w, lora_a)
    r, dec, k2, v, a_vec, b_vec, g = rwkv_prep(
        proj, prev, mu, rwkv_w0, _pad_rows(rwkv_w2, LANE).astype(BF16), rwkv_a0,
        _pad_rows(rwkv_a2, LANE).astype(BF16), rwkv_g2.astype(BF16), rwkv_k_k, rwkv_k_a)
    seqs = (r, dec, k2, v, a_vec, b_vec)
    s0_p = jnp.zeros((n_b, RWKV_HEAD_DIM, W), F32)
    s0_s = state_rwkv.astype(F32).transpose(0, 2, 1, 3).reshape(n_dec, RWKV_HEAD_DIM, W)
    o_p, st_p = rwkv_scan(seqs, 0, n_b, seq, s0_p)
    o_s, st_s = rwkv_scan(seqs, n_p, n_dec, dec_seq, s0_s)
    o = jnp.concatenate([o_p, o_s], axis=0)
    rwkv_out = rwkv_post(o, r, k2, v, g, rwkv_r_k, rwkv_gn_w, rwkv_gn_b)
    unstate = lambda s: s.reshape(-1, RWKV_HEAD_DIM, RWKV_HEADS, RWKV_HEAD_DIM).transpose(0, 2, 1, 3)

    w_out_b = w_out.astype(BF16)
    h1 = fused_matmul([fox_out, rwkv_out], [w_out_b[:FOX_WIDTH], w_out_b[FOX_WIDTH:]], residual=x,
                      name="proj_out")

    mem_kv = fused_matmul([mem_prompt.reshape(-1, d_model)], [w_mem_kv.astype(BF16)],
                          norm_gain=norm_mem_src,
                          head_gain=jnp.concatenate([jnp.tile(mem_k_gain, MEM_HEADS),
                                                     jnp.ones((MEM_WIDTH,), F32)]),
                          n_head_tiles=1, tn=MEM_WIDTH, name="mem_kv")
    n_mem = mem_prompt.shape[1]
    mem_k_p = mem_kv[:, :MEM_WIDTH].reshape(n_b, n_mem, MEM_WIDTH)
    mem_v_p = mem_kv[:, MEM_WIDTH:].reshape(n_b, n_mem, MEM_WIDTH)
    q_mem = fused_matmul([h1], [w_mem_q.astype(BF16)], norm_gain=norm_mem_q,
                         head_gain=jnp.tile(mem_q_gain, MEM_HEADS), n_head_tiles=1, tn=MEM_WIDTH,
                         name="mem_q")
    att_p = memory_attention(q_mem, 0, n_b, seq, mem_k_p, mem_v_p)
    att_s = memory_attention(q_mem, n_p, n_dec, dec_seq,
                             cache_mem_k.reshape(n_dec, -1, MEM_WIDTH),
                             cache_mem_v.reshape(n_dec, -1, MEM_WIDTH))
    h2 = fused_matmul([jnp.concatenate([att_p, att_s], axis=0)], [w_mem_o.astype(BF16)],
                      residual=h1, name="mem_out")

    w_router = _pad_cols(jnp.concatenate([w_router_group, w_router_expert], axis=1), LANE)
    logits, xn = fused_matmul([h2], [w_router], norm_gain=norm_ffn, emit_lhs=True, exact=True,
                              tn=LANE, name="moe_logits")
    idx, wts = moe_router(logits)
    slot_tok, block_expert, slots, n_used, n_blocks = moe_dispatch_tables(idx[:, :TOP_K], n_tok)
    xn_pad = jnp.concatenate([xn, jnp.zeros((8, d_model), F32)], axis=0)
    xs = moe_permute(xn_pad, slot_tok, n_used, n_blocks)
    yb = moe_experts(xs, w_exp_gate.astype(BF16), w_exp_up.astype(BF16), w_exp_down.astype(BF16),
                     block_expert, n_used, n_blocks)
    y = moe_combine(yb, h2, wts, slots)

    heads = lambda t, b, s: t.reshape(b, s, FOX_HEADS, FOX_HEAD_DIM)
    p_rwkv_last_p = _rwkv_slots_to_cols(p_rwkv[n_p - 1:n_p], lora_w, lora_a)
    p_rwkv_last_s = _rwkv_slots_to_cols(p_s[:, -1], lora_w, lora_a)
    return (y[:n_p].reshape(n_b, seq, d_model), y[n_p:].reshape(n_dec, dec_seq, d_model),
            heads(proj[:n_p, COL_K:COL_K + FOX_WIDTH], n_b, seq),
            heads(proj[:n_p, COL_V:COL_V + FOX_WIDTH], n_b, seq),
            logf_p.reshape(n_b, seq, FOX_HEADS),
            unstate(st_p), p_rwkv_last_p,
            mem_k_p.reshape(n_b, n_mem, MEM_HEADS, MEM_HEAD_DIM),
            mem_v_p.reshape(n_b, n_mem, MEM_HEADS, MEM_HEAD_DIM),
            heads(proj[n_p:, COL_K:COL_K + FOX_WIDTH], n_dec, dec_seq),
            heads(proj[n_p:, COL_V:COL_V + FOX_WIDTH], n_dec, dec_seq),
            logf_s.reshape(n_dec, dec_seq, FOX_HEADS),
            unstate(st_s), p_rwkv_last_s)
```

```python
import functools

import numpy as np
import jax
import jax.numpy as jnp
from jax import lax
from jax.experimental import pallas as pl
from jax.experimental.pallas import tpu as pltpu

F32 = jnp.float32
BF16 = jnp.bfloat16
HI = lax.Precision.HIGHEST

LANE = 128
FOX_HEADS = 16
FOX_HEAD_DIM = 128
FOX_WIDTH = FOX_HEADS * FOX_HEAD_DIM
RWKV_HEADS = 32
RWKV_HEAD_DIM = 64
RWKV_WIDTH = RWKV_HEADS * RWKV_HEAD_DIM
MEM_HEADS = 4
MEM_HEAD_DIM = 128
MEM_WIDTH = MEM_HEADS * MEM_HEAD_DIM
N_GROUPS = 8
EXPERTS_PER_GROUP = 8
N_EXPERTS = N_GROUPS * EXPERTS_PER_GROUP
TOP_K = 2
EXPERT_BLOCK = 128
NORM_EPS = 1e-6
RWKV_GN_EPS = 64e-5
FOX_SCALE = FOX_HEAD_DIM ** -0.5
MEM_SCALE = MEM_HEAD_DIM ** -0.5
VMEM_LIMIT = 56 * 1024 * 1024

COL_Q = 0
COL_K = FOX_WIDTH
COL_V = 2 * FOX_WIDTH
COL_RWKV = 3 * FOX_WIDTH
LORA_SLOTS = 2 * LANE + 256
RWKV_COLS_PADDED = 3 * RWKV_WIDTH + LORA_SLOTS
COL_F = COL_RWKV + RWKV_COLS_PADDED
PROJ_COLS = COL_F + LANE
PROJ_TN = 512
PROJ_COLS_PADDED = -(-PROJ_COLS // PROJ_TN) * PROJ_TN


def _cparams(sem):
    return pltpu.CompilerParams(dimension_semantics=sem, vmem_limit_bytes=VMEM_LIMIT)


def _dot(a, b, precision=None):
    return jnp.dot(a, b, preferred_element_type=F32, precision=precision)


def _dot_nt(a, b, precision=None):
    return lax.dot_general(a, b, (((1,), (1,)), ((), ())), preferred_element_type=F32,
                           precision=precision)


def _iota(shape, dim):
    return lax.broadcasted_iota(jnp.int32, shape, dim)


def _mm_body(*refs, n_lhs, has_norm, has_res, n_head_tiles, emit_lhs, exact, tn):
    it = iter(refs)
    lhs = [next(it) for _ in range(n_lhs)]
    gain = next(it) if has_norm else None
    ws = [next(it) for _ in range(n_lhs)]
    res = next(it) if has_res else None
    hgain = next(it) if n_head_tiles else None
    out = next(it)
    lhs_out = next(it) if emit_lhs else None
    xs = [next(it) for _ in range(n_lhs)]
    j = pl.program_id(1)

    @pl.when(j == 0)
    def _():
        for l_ref, x_ref in zip(lhs, xs):
            v = l_ref[...]
            if has_norm:
                v = v * lax.rsqrt(jnp.mean(v * v, axis=-1, keepdims=True) + NORM_EPS) * gain[...]
            x_ref[...] = v.astype(x_ref.dtype)
            if emit_lhs:
                lhs_out[...] = v

    acc = None
    for x_ref, w_ref in zip(xs, ws):
        d = _dot(x_ref[...], w_ref[...], HI if exact else None)
        acc = d if acc is None else acc + d
    if has_res:
        acc = acc + res[...]
    if n_head_tiles:
        @pl.when(j < n_head_tiles)
        def _():
            g = hgain[...]
            for s in range(tn // LANE):
                seg = acc[:, s * LANE:(s + 1) * LANE]
                ms = jnp.mean(seg * seg, axis=-1, keepdims=True)
                out[:, s * LANE:(s + 1) * LANE] = (seg * lax.rsqrt(ms + NORM_EPS)
                                                   * g[:, s * LANE:(s + 1) * LANE])

        @pl.when(j >= n_head_tiles)
        def _():
            out[...] = acc
    else:
        out[...] = acc


def fused_matmul(lhs, ws, *, norm_gain=None, residual=None, head_gain=None, n_head_tiles=0,
                 emit_lhs=False, exact=False, tm=512, tn=512, name="fused_matmul"):
    n_lhs = len(lhs)
    m = lhs[0].shape[0]
    n = ws[0].shape[1]
    tm = min(tm, m)
    tn = min(tn, n)
    assert m % tm == 0 and n % tn == 0
    has_norm = norm_gain is not None
    has_res = residual is not None
    assert not (has_norm and n_lhs != 1)
    in_specs, args = [], []
    for l in lhs:
        in_specs.append(pl.BlockSpec((tm, l.shape[1]), lambda i, j: (i, 0)))
        args.append(l)
    if has_norm:
        in_specs.append(pl.BlockSpec((1, lhs[0].shape[1]), lambda i, j: (0, 0)))
        args.append(norm_gain.reshape(1, -1).astype(F32))
    for w in ws:
        in_specs.append(pl.BlockSpec((w.shape[0], tn), lambda i, j: (0, j)))
        args.append(w)
    if has_res:
        in_specs.append(pl.BlockSpec((tm, tn), lambda i, j: (i, j)))
        args.append(residual)
    if n_head_tiles:
        in_specs.append(pl.BlockSpec((1, tn), lambda i, j: (0, j)))
        args.append(head_gain.reshape(1, -1).astype(F32))
    out_shape = [jax.ShapeDtypeStruct((m, n), F32)]
    out_specs = [pl.BlockSpec((tm, tn), lambda i, j: (i, j))]
    if emit_lhs:
        out_shape.append(jax.ShapeDtypeStruct((m, lhs[0].shape[1]), F32))
        out_specs.append(pl.BlockSpec((tm, lhs[0].shape[1]), lambda i, j: (i, 0)))
    scratch = [pltpu.VMEM((tm, l.shape[1]), F32 if exact else BF16) for l in lhs]
    body = functools.partial(_mm_body, n_lhs=n_lhs, has_norm=has_norm, has_res=has_res,
                             n_head_tiles=n_head_tiles, emit_lhs=emit_lhs, exact=exact, tn=tn)
    res = pl.pallas_call(
        body, grid=(m // tm, n // tn), in_specs=in_specs, out_specs=out_specs,
        out_shape=out_shape, scratch_shapes=scratch,
        compiler_params=_cparams(("parallel", "arbitrary")), name=name)(*args)
    return res if emit_lhs else res[0]


def _log_sigmoid(x):
    return jnp.minimum(x, 0.0) - jnp.log1p(jnp.exp(-jnp.abs(x)))


def _logf_body(f_ref, b_ref, logf_ref, c_ref, carry, *, tc):
    i = pl.program_id(0)

    @pl.when(i == 0)
    def _():
        carry[...] = jnp.zeros_like(carry)

    lf = _log_sigmoid(f_ref[:, :FOX_HEADS] + b_ref[...])
    logf_ref[...] = lf
    tri = (_iota((tc, tc), 0) >= _iota((tc, tc), 1)).astype(F32)
    c = _dot(tri, lf, HI) + carry[...]
    c_ref[...] = c
    carry[...] = c[tc - 1:tc, :]


def fox_logf_cumsum(proj, row0, rows, b_f, *, tc=512):
    tc = min(tc, rows)
    assert rows % tc == 0 and row0 % tc == 0
    rb0 = row0 // tc
    return pl.pallas_call(
        functools.partial(_logf_body, tc=tc), grid=(rows // tc,),
        in_specs=[pl.BlockSpec((tc, LANE), lambda i: (rb0 + i, COL_F // LANE)),
                  pl.BlockSpec((1, FOX_HEADS), lambda i: (0, 0))],
        out_specs=[pl.BlockSpec((tc, FOX_HEADS), lambda i: (i, 0)),
                   pl.BlockSpec((tc, FOX_HEADS), lambda i: (i, 0))],
        out_shape=[jax.ShapeDtypeStruct((rows, FOX_HEADS), F32)] * 2,
        scratch_shapes=[pltpu.VMEM((1, FOX_HEADS), F32)],
        compiler_params=_cparams(("arbitrary",)), name="fox_logf_cumsum")(
            proj, b_f.reshape(1, FOX_HEADS).astype(F32))


def _flash_body(qi_tbl, ki_tbl, q_ref, k_ref, v_ref, c_ref, ct_ref, o_ref, m_s, l_s, acc_s, cq_s,
                *, tq, tk):
    h = pl.program_id(0)
    s_idx = pl.program_id(1)
    qi = qi_tbl[s_idx]
    ki = ki_tbl[s_idx]

    @pl.when(ki == 0)
    def _():
        m_s[...] = jnp.full_like(m_s, -jnp.inf)
        l_s[...] = jnp.zeros_like(l_s)
        acc_s[...] = jnp.zeros_like(acc_s)
        lane = _iota((tq, FOX_HEADS), 1)
        cq_s[...] = jnp.sum(jnp.where(lane == h, c_ref[...], 0.0), axis=1, keepdims=True)

    def update(masked):
        s = _dot_nt(q_ref[...].astype(BF16), k_ref[...].astype(BF16)) * FOX_SCALE
        s = s + cq_s[...] - ct_ref[...]
        if masked:
            q_pos = qi * tq + _iota((tq, tk), 0)
            k_pos = ki * tk + _iota((tq, tk), 1)
            s = jnp.where(k_pos <= q_pos, s, -jnp.inf)
        m_prev = m_s[...]
        m_new = jnp.maximum(m_prev, jnp.max(s, axis=1, keepdims=True))
        alpha = jnp.exp(m_prev - m_new)
        p = jnp.exp(s - m_new)
        l_s[...] = alpha * l_s[...] + jnp.sum(p, axis=1, keepdims=True)
        acc_s[...] = alpha * acc_s[...] + _dot(p.astype(BF16), v_ref[...].astype(BF16))
        m_s[...] = m_new

    crosses = (ki + 1) * tk - 1 > qi * tq

    @pl.when(crosses)
    def _():
        update(True)

    @pl.when(jnp.logical_not(crosses))
    def _():
        update(False)

    @pl.when(ki == (qi + 1) * (tq // tk) - 1)
    def _():
        o_ref[...] = acc_s[...] / l_s[...]


def fox_prompt_attention(proj, c, seq, *, tq=1024, tk=1024):
    tq = min(tq, seq)
    tk = min(tk, tq)
    assert seq % tq == 0 and tq % tk == 0
    nq = seq // tq
    qi_np, ki_np = [], []
    for qi in range(nq):
        for ki in range((qi + 1) * (tq // tk)):
            qi_np.append(qi)
            ki_np.append(ki)
    qi_tbl = jnp.asarray(np.array(qi_np, np.int32))
    ki_tbl = jnp.asarray(np.array(ki_np, np.int32))
    ct = c.T.reshape(FOX_HEADS, 1, seq)
    grid_spec = pltpu.PrefetchScalarGridSpec(
        num_scalar_prefetch=2, grid=(FOX_HEADS, len(qi_np)),
        in_specs=[
            pl.BlockSpec((tq, FOX_HEAD_DIM), lambda h, s, qt, kt: (qt[s], COL_Q // LANE + h)),
            pl.BlockSpec((tk, FOX_HEAD_DIM), lambda h, s, qt, kt: (kt[s], COL_K // LANE + h)),
            pl.BlockSpec((tk, FOX_HEAD_DIM), lambda h, s, qt, kt: (kt[s], COL_V // LANE + h)),
            pl.BlockSpec((tq, FOX_HEADS), lambda h, s, qt, kt: (qt[s], 0)),
            pl.BlockSpec((None, 1, tk), lambda h, s, qt, kt: (h, 0, kt[s])),
        ],
        out_specs=pl.BlockSpec((tq, FOX_HEAD_DIM), lambda h, s, qt, kt: (qt[s], h)),
        scratch_shapes=[pltpu.VMEM((tq, 1), F32), pltpu.VMEM((tq, 1), F32),
                        pltpu.VMEM((tq, FOX_HEAD_DIM), F32), pltpu.VMEM((tq, 1), F32)])
    return pl.pallas_call(
        functools.partial(_flash_body, tq=tq, tk=tk), grid_spec=grid_spec,
        out_shape=jax.ShapeDtypeStruct((seq, FOX_WIDTH), F32),
        compiler_params=_cparams(("parallel", "arbitrary")), name="fox_prompt_attention")(
            qi_tbl, ki_tbl, proj, proj, proj, c, ct)


def _decode_body(pt_ref, q_ref, kn_ref, vn_ref, lfn_ref, kc_ref, vc_ref, lfc_ref, o_ref,
                 q_s, m_s, l_s, acc_s, carry_s, ccol_s, *, n_pages, tnew, page):
    p = pl.program_id(1)
    H = FOX_HEADS
    D = FOX_HEAD_DIM
    rows = H * tnew
    n_lane = page * H
    SUB = 8

    def head_rows(x):
        return jnp.concatenate([x[:, h * D:(h + 1) * D] for h in range(H)], axis=0)

    @pl.when(p == 0)
    def _():
        q2 = head_rows(q_ref[...]).astype(BF16)
        q_s[...] = q2
        kn = head_rows(kn_ref[...]).astype(BF16)
        vn = head_rows(vn_ref[...]).astype(BF16)
        spread = (_iota((H, rows), 0) == _iota((H, rows), 1) // tnew).astype(F32)
        lf_exp = _dot(lfn_ref[...], spread, HI)
        upto = _iota((tnew, rows), 0) <= _iota((tnew, rows), 1) % tnew
        cn_row = jnp.sum(jnp.where(upto, lf_exp, 0.0), axis=0, keepdims=True)
        ri = _iota((rows, rows), 0)
        li = _iota((rows, rows), 1)
        ccol = jnp.sum(jnp.where(ri == li, cn_row, 0.0), axis=1, keepdims=True)
        ccol_s[...] = ccol
        s = _dot_nt(q2, kn) * FOX_SCALE + ccol - cn_row
        valid = jnp.logical_and(ri // tnew == li // tnew, li % tnew <= ri % tnew)
        s = jnp.where(valid, s, -jnp.inf)
        m = jnp.max(s, axis=1, keepdims=True)
        pr = jnp.exp(s - m)
        m_s[...] = m
        l_s[...] = jnp.sum(pr, axis=1, keepdims=True)
        acc_s[...] = _dot(pr.astype(BF16), vn)
        carry_s[...] = jnp.zeros_like(carry_s)

    lane = _iota((SUB, n_lane), 1)
    lf = jnp.broadcast_to(lfc_ref[...], (SUB, n_lane))

    def later(x, sh):
        return jnp.where(lane < n_lane - sh, pltpu.roll(x, n_lane - sh, 1), 0.0)

    total = lf
    sfx = later(lf, H)
    for kk in range(page.bit_length() - 1):
        total = total + pltpu.roll(total, H << kk, 1)
        sfx = sfx + later(sfx, H << kk)
    bias = (sfx + carry_s[...])[0:1, :]
    s = _dot_nt(q_s[...], kc_ref[...].astype(BF16)) * FOX_SCALE
    s = s + ccol_s[...] + bias
    same_head = _iota((rows, n_lane), 1) % H == _iota((rows, n_lane), 0) // tnew
    s = jnp.where(same_head, s, -jnp.inf)
    m_prev = m_s[...]
    m_new = jnp.maximum(m_prev, jnp.max(s, axis=1, keepdims=True))
    alpha = jnp.exp(m_prev - m_new)
    pr = jnp.exp(s - m_new)
    l_s[...] = alpha * l_s[...] + jnp.sum(pr, axis=1, keepdims=True)
    acc_s[...] = alpha * acc_s[...] + _dot(pr.astype(BF16), vc_ref[...].astype(BF16))
    m_s[...] = m_new
    carry_s[...] = carry_s[...] + total

    @pl.when(p == n_pages - 1)
    def _():
        o = acc_s[...] / l_s[...]
        o_ref[...] = jnp.concatenate([o[h * tnew:(h + 1) * tnew, :] for h in range(H)], axis=1)


def fox_decode_attention(proj, row0, logf_new, cache_k, cache_v, cache_logf, page_table, tnew):
    n_seq, n_pages = page_table.shape
    n_pool, page = cache_k.shape[0], cache_k.shape[1]
    assert row0 % tnew == 0 and tnew == 8 and page & (page - 1) == 0
    rb0 = row0 // tnew
    kc = cache_k.reshape(n_pool, page * FOX_HEADS, FOX_HEAD_DIM)
    vc = cache_v.reshape(n_pool, page * FOX_HEADS, FOX_HEAD_DIM)
    cache_logf = cache_logf.reshape(n_pool, 1, page * FOX_HEADS)
    rows = FOX_HEADS * tnew
    paged = lambda b, p, pt: (pt[b, n_pages - 1 - p], 0, 0)
    grid_spec = pltpu.PrefetchScalarGridSpec(
        num_scalar_prefetch=1, grid=(n_seq, n_pages),
        in_specs=[
            pl.BlockSpec((tnew, FOX_WIDTH), lambda b, p, pt: (rb0 + b, COL_Q // FOX_WIDTH)),
            pl.BlockSpec((tnew, FOX_WIDTH), lambda b, p, pt: (rb0 + b, COL_K // FOX_WIDTH)),
            pl.BlockSpec((tnew, FOX_WIDTH), lambda b, p, pt: (rb0 + b, COL_V // FOX_WIDTH)),
            pl.BlockSpec((tnew, FOX_HEADS), lambda b, p, pt: (b, 0)),
            pl.BlockSpec((None, page * FOX_HEADS, FOX_HEAD_DIM), paged),
            pl.BlockSpec((None, page * FOX_HEADS, FOX_HEAD_DIM), paged),
            pl.BlockSpec((None, 1, page * FOX_HEADS), paged),
        ],
        out_specs=pl.BlockSpec((tnew, FOX_WIDTH), lambda b, p, pt: (b, 0)),
        scratch_shapes=[pltpu.VMEM((rows, FOX_HEAD_DIM), BF16), pltpu.VMEM((rows, 1), F32),
                        pltpu.VMEM((rows, 1), F32), pltpu.VMEM((rows, FOX_HEAD_DIM), F32),
                        pltpu.VMEM((8, page * FOX_HEADS), F32), pltpu.VMEM((rows, 1), F32)])
    return pl.pallas_call(
        functools.partial(_decode_body, n_pages=n_pages, tnew=tnew, page=page),
        grid_spec=grid_spec, out_shape=jax.ShapeDtypeStruct((n_seq * tnew, FOX_WIDTH), F32),
        compiler_params=_cparams(("parallel", "arbitrary")), name="fox_decode_attention")(
            page_table, proj, proj, proj, logf_new, kc, vc, cache_logf)


def _head_sum_matrix(width):
    return (_iota((width, width), 0) // RWKV_HEAD_DIM
            == _iota((width, width), 1) // RWKV_HEAD_DIM)


def _head_sum_exact(x):
    pm = _head_sum_matrix(LANE).astype(F32)
    return jnp.concatenate([_dot(x[:, j * LANE:(j + 1) * LANE], pm, HI)
                            for j in range(x.shape[1] // LANE)], axis=1)


def _softplus(y):
    return jnp.maximum(y, 0.0) + jnp.log1p(jnp.exp(-jnp.abs(y)))


def _rwkv_prep_body(pr_ref, pk_ref, pv_ref, pl_ref, qr_ref, qk_ref, qv_ref, ql_ref,
                    mr_ref, mk_ref, mv_ref, ml_ref, w0_ref, w2_ref, a0_ref, a2_ref, g2_ref,
                    kk_ref, ka_ref, r_o, w_o, k_o, v_o, a_o, b_o, g_o):
    def shift(p_ref, q_ref, m_ref):
        p = p_ref[...]
        return p + m_ref[...] * (q_ref[...] - p)

    r = shift(pr_ref, qr_ref, mr_ref)
    k = shift(pk_ref, qk_ref, mk_ref)
    v = shift(pv_ref, qv_ref, mv_ref)
    zl = shift(pl_ref, ql_ref, ml_ref)
    xw = zl[:, 0:LANE]
    xa = zl[:, LANE:2 * LANE]
    xg = zl[:, 2 * LANE:]
    lw = w0_ref[...] + _dot(jnp.tanh(xw).astype(BF16), w2_ref[...])
    w_log = -_softplus(-lw) - 0.5
    a = jax.nn.sigmoid(a0_ref[...] + _dot(xa.astype(BF16), a2_ref[...]))
    kk = k * kk_ref[...]
    kk = kk * lax.rsqrt(jnp.maximum(_head_sum_exact(kk * kk), 1e-24))
    r_o[...] = r
    w_o[...] = jnp.exp(-jnp.exp(w_log))
    k_o[...] = k * (1.0 + (a - 1.0) * ka_ref[...])
    v_o[...] = v
    a_o[...] = -kk
    b_o[...] = kk * a
    g_o[...] = _dot(jax.nn.sigmoid(xg).astype(BF16), g2_ref[...])


def rwkv_prep(proj, prev, mu, w0, w2, a0, a2, g2, k_k, k_a, *, tm=128):
    m = proj.shape[0]
    tm = min(tm, m)
    W = RWKV_WIDTH
    assert m % tm == 0 and COL_RWKV % W == 0 and (COL_RWKV + 3 * W) % LORA_SLOTS == 0
    row = lambda a: a.reshape(1, -1).astype(F32)
    vec = pl.BlockSpec((1, W), lambda i: (0, 0))

    def pieces(rows, col0):
        specs = [pl.BlockSpec((rows, W), functools.partial(lambda i, c: (i if rows > 1 else 0, c),
                                                           c=col0 // W + n)) for n in range(3)]
        specs.append(pl.BlockSpec((rows, LORA_SLOTS),
                                  lambda i: (i if rows > 1 else 0, (col0 + 3 * W) // LORA_SLOTS)))
        return specs

    return pl.pallas_call(
        _rwkv_prep_body, grid=(m // tm,),
        in_specs=pieces(tm, COL_RWKV) + pieces(tm, 0) + pieces(1, 0) + [
            vec, pl.BlockSpec((LANE, W), lambda i: (0, 0)), vec,
            pl.BlockSpec((LANE, W), lambda i: (0, 0)),
            pl.BlockSpec((256, W), lambda i: (0, 0)), vec, vec],
        out_specs=[pl.BlockSpec((tm, W), lambda i: (i, 0))] * 7,
        out_shape=[jax.ShapeDtypeStruct((m, W), F32)] * 7,
        compiler_params=_cparams(("parallel",)), name="rwkv_prep")(
            *([proj] * 4 + [prev] * 4 + [mu] * 4),
            row(w0), w2, row(a0), a2, g2, row(k_k), row(k_a))


def _scan_body(r_ref, w_ref, k_ref, v_ref, a_ref, b_ref, s0_ref, o_ref, st_ref, state, *, tc,
               n_chunks, n_par, unroll):
    c = pl.program_id(1)
    N = RWKV_HEAD_DIM
    W = RWKV_WIDTH
    SEG = 256
    n_seg = W // SEG

    @pl.when(c == 0)
    def _():
        state[...] = s0_ref[...]

    diag = (_iota((N, W), 1) % N == _iota((N, W), 0)).astype(F32)
    pm = _head_sum_matrix(SEG).astype(BF16)

    def head_sums(parts):
        n = len(parts)
        tiles = [p[:, j * SEG:(j + 1) * SEG] for j in range(n_seg) for p in parts]
        y = _dot(jnp.concatenate(tiles, axis=0), pm)
        return [jnp.concatenate([y[(j * n + i) * N:(j * n + i + 1) * N, :]
                                 for j in range(n_seg)], axis=1) for i in range(n)]

    def row(ref, g, t):
        return ref[pl.ds(g * tc + t, 1), :]

    def out_row(o_bc):
        return jnp.sum(o_bc * diag, axis=0, keepdims=True)

    def advance(t, with_prev):
        parts, n_each = [], 3 if with_prev else 2
        for g in range(n_par):
            s = state[g]
            parts += [(s * row(a_ref, g, t)).astype(BF16), (row(v_ref, g, t) * diag).astype(BF16)]
            if with_prev:
                parts.append((s * row(r_ref, g, t - 1)).astype(BF16))
        sums = head_sums(parts)
        for g in range(n_par):
            sa, vc = sums[g * n_each:g * n_each + 2]
            state[g] = (state[g] * row(w_ref, g, t) + sa * row(b_ref, g, t)
                        + vc * row(k_ref, g, t))
            if with_prev:
                o_ref[pl.ds(g * tc + t - 1, 1), :] = out_row(sums[g * n_each + 2])

    advance(0, False)

    def step(t, carry):
        advance(t, True)
        return carry

    lax.fori_loop(1, tc, step, 0, unroll=unroll)
    last = head_sums([(state[g] * row(r_ref, g, tc - 1)).astype(BF16) for g in range(n_par)])
    for g in range(n_par):
        o_ref[pl.ds(g * tc + tc - 1, 1), :] = out_row(last[g])

    @pl.when(c == n_chunks - 1)
    def _():
        st_ref[...] = state[...]


def rwkv_scan(seqs, row0, n_seq, seq_len, s0, *, tc=256, n_par=1, unroll=2):
    tc = min(tc, seq_len)
    n_chunks = seq_len // tc
    assert seq_len % tc == 0 and row0 % (n_par * tc) == 0 and n_seq % n_par == 0
    assert n_par == 1 or n_chunks == 1
    rb0 = row0 // (n_par * tc)
    W = RWKV_WIDTH
    tok = pl.BlockSpec((n_par * tc, W), lambda s, c: (rb0 + s * n_chunks + c, 0))
    st = pl.BlockSpec((n_par, RWKV_HEAD_DIM, W), lambda s, c: (s, 0, 0))
    return pl.pallas_call(
        functools.partial(_scan_body, tc=tc, n_chunks=n_chunks, n_par=n_par,
                          unroll=min(unroll, tc - 1)),
        grid=(n_seq // n_par, n_chunks),
        in_specs=[tok] * 6 + [st],
        out_specs=[pl.BlockSpec((n_par * tc, W), lambda s, c: (s * n_chunks + c, 0)), st],
        out_shape=[jax.ShapeDtypeStruct((n_seq * seq_len, W), F32),
                   jax.ShapeDtypeStruct((n_seq, RWKV_HEAD_DIM, W), F32)],
        scratch_shapes=[pltpu.VMEM((n_par, RWKV_HEAD_DIM, W), F32)],
        compiler_params=_cparams(("parallel", "arbitrary")), name="rwkv_scan")(*seqs, s0)


def _rwkv_post_body(o_ref, r_ref, k_ref, v_ref, g_ref, rk_ref, gw_ref, gb_ref, out_ref):
    o = o_ref[...]
    inv_n = 1.0 / RWKV_HEAD_DIM
    mean = _head_sum_exact(o) * inv_n
    d = o - mean
    var = _head_sum_exact(d * d) * inv_n
    y = d * lax.rsqrt(var + RWKV_GN_EPS) * gw_ref[...] + gb_ref[...]
    y = y + _head_sum_exact(r_ref[...] * k_ref[...] * rk_ref[...]) * v_ref[...]
    out_ref[...] = y * g_ref[...]


def rwkv_post(o, r, k, v, g, r_k, gn_w, gn_b, *, tm=256):
    m = o.shape[0]
    tm = min(tm, m)
    assert m % tm == 0
    W = RWKV_WIDTH
    row = lambda a: a.reshape(1, -1).astype(F32)
    tok = pl.BlockSpec((tm, W), lambda i: (i, 0))
    vec = pl.BlockSpec((1, W), lambda i: (0, 0))
    return pl.pallas_call(
        _rwkv_post_body, grid=(m // tm,), in_specs=[tok] * 5 + [vec] * 3, out_specs=tok,
        out_shape=jax.ShapeDtypeStruct((m, W), F32),
        compiler_params=_cparams(("parallel",)), name="rwkv_post")(
            o, r, k, v, g, row(r_k), row(gn_w), row(gn_b))


def _memattn_body(q_ref, k_ref, v_ref, o_ref):
    q = q_ref[...]
    k = k_ref[...]
    v = v_ref[...]
    outs = []
    for h in range(MEM_HEADS):
        sl = slice(h * MEM_HEAD_DIM, (h + 1) * MEM_HEAD_DIM)
        s = _dot_nt(q[:, sl].astype(BF16), k[:, sl].astype(BF16)) * MEM_SCALE
        e = jnp.exp(s - jnp.max(s, axis=1, keepdims=True))
        pr = e / jnp.sum(e, axis=1, keepdims=True)
        outs.append(_dot(pr.astype(BF16), v[:, sl].astype(BF16)))
    o_ref[...] = jnp.concatenate(outs, axis=1)


def memory_attention(q, row0, n_seq, seq_len, mem_k, mem_v, *, tq=512):
    tq = min(tq, seq_len)
    assert seq_len % tq == 0 and row0 % tq == 0
    nt = seq_len // tq
    rb0 = row0 // tq
    m_tok = mem_k.shape[1]
    kv = pl.BlockSpec((None, m_tok, MEM_WIDTH), lambda b, t: (b, 0, 0))
    return pl.pallas_call(
        _memattn_body, grid=(n_seq, nt),
        in_specs=[pl.BlockSpec((tq, MEM_WIDTH), lambda b, t: (rb0 + b * nt + t, 0)), kv, kv],
        out_specs=pl.BlockSpec((tq, MEM_WIDTH), lambda b, t: (b * nt + t, 0)),
        out_shape=jax.ShapeDtypeStruct((n_seq * seq_len, MEM_WIDTH), F32),
        compiler_params=_cparams(("parallel", "parallel")), name="memory_attention")(
            q, mem_k, mem_v)


def _router_body(lg_ref, idx_ref, wt_ref):
    lg = lg_ref[...]
    shape = lg.shape
    lane = _iota(shape, 1).astype(F32)
    neg = -jnp.inf
    big = float(LANE)
    is_g = lane < N_GROUPS
    gl = jnp.where(is_g, lg, neg)
    gmax = jnp.max(gl, axis=1, keepdims=True)
    gsel = jnp.min(jnp.where(gl == gmax, lane, big), axis=1, keepdims=True)
    gsum = jnp.sum(jnp.where(is_g, jnp.exp(lg - gmax), 0.0), axis=1, keepdims=True)
    g_w = 1.0 / gsum
    base = N_GROUPS + gsel * EXPERTS_PER_GROUP
    in_g = jnp.logical_and(lane >= base, lane < base + EXPERTS_PER_GROUP)
    el = jnp.where(in_g, lg, neg)
    emax = jnp.max(el, axis=1, keepdims=True)
    ee = jnp.where(in_g, jnp.exp(lg - emax), 0.0)
    prob = ee / jnp.sum(ee, axis=1, keepdims=True)
    pm = jnp.where(in_g, prob, -1.0)
    p1 = jnp.max(pm, axis=1, keepdims=True)
    i1 = jnp.min(jnp.where(pm == p1, lane, big), axis=1, keepdims=True)
    pm2 = jnp.where(lane == i1, -1.0, pm)
    p2 = jnp.max(pm2, axis=1, keepdims=True)
    i2 = jnp.min(jnp.where(pm2 == p2, lane, big), axis=1, keepdims=True)
    tot = p1 + p2
    idx_ref[...] = jnp.where(lane == 0, i1 - N_GROUPS,
                             jnp.where(lane == 1, i2 - N_GROUPS, 0.0)).astype(jnp.int32)
    wt_ref[...] = jnp.where(lane == 0, g_w * (p1 / tot), jnp.where(lane == 1, g_w * (p2 / tot), 0.0))


def moe_router(logits, *, tm=512):
    m = logits.shape[0]
    tm = min(tm, m)
    assert m % tm == 0
    blk = pl.BlockSpec((tm, LANE), lambda i: (i, 0))
    return pl.pallas_call(
        _router_body, grid=(m // tm,), in_specs=[blk], out_specs=[blk, blk],
        out_shape=[jax.ShapeDtypeStruct((m, LANE), jnp.int32), jax.ShapeDtypeStruct((m, LANE), F32)],
        compiler_params=_cparams(("parallel",)), name="moe_router")(logits)


def _experts_body(be_ref, nused_ref, tok_ref, x_hbm, wg_ref, wu_ref, wd_ref, y_ref, xbuf, sem):
    i = pl.program_id(0)
    n_used = nused_ref[0]

    def gather(blk, buf):
        base = blk * EXPERT_BLOCK

        def issue(r, carry):
            pltpu.make_async_copy(x_hbm.at[pl.ds(tok_ref[base + r], 1)],
                                  xbuf.at[buf, pl.ds(r, 1)], sem.at[buf]).start()
            return carry

        lax.fori_loop(0, EXPERT_BLOCK, issue, 0)

    @pl.when(i == 0)
    def _():
        gather(0, 0)

    @pl.when(i < n_used)
    def _():
        buf = i % 2

        @pl.when(i + 1 < n_used)
        def _():
            gather(i + 1, 1 - buf)

        pltpu.make_async_copy(x_hbm.at[pl.ds(0, EXPERT_BLOCK)], xbuf.at[buf], sem.at[buf]).wait()
        x = xbuf[buf].astype(BF16)
        hb = jax.nn.silu(_dot(x, wg_ref[...])) * _dot(x, wu_ref[...])
        y_ref[...] = _dot(hb.astype(BF16), wd_ref[...])


def moe_experts(x_pad, slot_tok, wg, wu, wd, block_expert, n_used, n_blocks):
    d = x_pad.shape[1]
    de = wg.shape[2]
    rows = lambda i, be, nu, tk: (jnp.minimum(i, nu[0] - 1), 0)
    expert = lambda i, be, nu, tk: (be[i], 0, 0)
    grid_spec = pltpu.PrefetchScalarGridSpec(
        num_scalar_prefetch=3, grid=(n_blocks,),
        in_specs=[pl.BlockSpec(memory_space=pl.ANY),
                  pl.BlockSpec((None, d, de), expert),
                  pl.BlockSpec((None, d, de), expert),
                  pl.BlockSpec((None, de, d), expert)],
        out_specs=pl.BlockSpec((EXPERT_BLOCK, d), rows),
        scratch_shapes=[pltpu.VMEM((2, EXPERT_BLOCK, d), F32), pltpu.SemaphoreType.DMA((2,))])
    return pl.pallas_call(
        _experts_body, grid_spec=grid_spec,
        out_shape=jax.ShapeDtypeStruct((n_blocks * EXPERT_BLOCK, d), F32),
        compiler_params=_cparams(("arbitrary",)), name="moe_experts")(
            block_expert, n_used, slot_tok, x_pad, wg, wu, wd)


def _combine_body(slot_ref, yb_hbm, h_ref, w_ref, y_ref, g_s, sem, *, tm):
    i = pl.program_id(0)

    def issue(r, carry):
        tok = i * tm + r
        for kk in range(TOP_K):
            pltpu.make_async_copy(yb_hbm.at[pl.ds(slot_ref[TOP_K * tok + kk], 1)],
                                  g_s.at[kk, pl.ds(r, 1)], sem).start()
        return carry

    lax.fori_loop(0, tm, issue, 0)
    for kk in range(TOP_K):
        pltpu.make_async_copy(yb_hbm.at[pl.ds(0, tm)], g_s.at[kk], sem).wait()
    w = w_ref[...]
    y_ref[...] = h_ref[...] + (g_s[0] * w[:, 0:1] + g_s[1] * w[:, 1:2])


def moe_combine(yb, h, wts, slots, *, tm=128):
    m, d = h.shape
    tm = min(tm, m)
    assert m % tm == 0
    grid_spec = pltpu.PrefetchScalarGridSpec(
        num_scalar_prefetch=1, grid=(m // tm,),
        in_specs=[pl.BlockSpec(memory_space=pl.ANY),
                  pl.BlockSpec((tm, d), lambda i, s: (i, 0)),
                  pl.BlockSpec((tm, LANE), lambda i, s: (i, 0))],
        out_specs=pl.BlockSpec((tm, d), lambda i, s: (i, 0)),
        scratch_shapes=[pltpu.VMEM((TOP_K, tm, d), F32), pltpu.SemaphoreType.DMA(())])
    return pl.pallas_call(
        functools.partial(_combine_body, tm=tm), grid_spec=grid_spec,
        out_shape=jax.ShapeDtypeStruct((m, d), F32),
        compiler_params=_cparams(("arbitrary",)), name="moe_combine")(slots, yb, h, wts)


def moe_dispatch_tables(expert_idx, n_tok):
    n_assign = n_tok * TOP_K
    flat_e = expert_idx.reshape(n_assign).astype(jnp.int32)
    flat_tok = jnp.repeat(jnp.arange(n_tok, dtype=jnp.int32), TOP_K)
    order = jnp.argsort(flat_e)
    e_sorted = flat_e[order]
    counts = jnp.zeros((N_EXPERTS,), jnp.int32).at[flat_e].add(1)
    padded = (counts + EXPERT_BLOCK - 1) // EXPERT_BLOCK * EXPERT_BLOCK
    pad_end = jnp.cumsum(padded)
    pad_start = pad_end - padded
    start = jnp.cumsum(counts) - counts
    dest = pad_start[e_sorted] + jnp.arange(n_assign, dtype=jnp.int32) - start[e_sorted]
    n_blocks = -(-n_assign // EXPERT_BLOCK) + N_EXPERTS
    n_slots = n_blocks * EXPERT_BLOCK
    slot_tok = jnp.full((n_slots,), n_tok, jnp.int32).at[dest].set(flat_tok[order])
    block_start = jnp.arange(n_blocks, dtype=jnp.int32) * EXPERT_BLOCK
    block_expert = jnp.minimum(jnp.searchsorted(pad_end, block_start, side='right'),
                               N_EXPERTS - 1).astype(jnp.int32)
    slots = jnp.zeros((n_assign,), jnp.int32).at[order].set(dest)
    n_used = (pad_end[-1:] // EXPERT_BLOCK).astype(jnp.int32)
    return slot_tok, block_expert, slots, n_used, n_blocks


def _pad_cols(a, width):
    return jnp.pad(a, ((0, 0), (0, width - a.shape[1])))


def _pad_rows(a, height):
    return jnp.pad(a, ((0, height - a.shape[0]), (0, 0)))


def _rwkv_cols_to_slots(a, lora_w, lora_a):
    W = RWKV_WIDTH
    return jnp.concatenate([a[:, :3 * W], _pad_cols(a[:, 3 * W:3 * W + lora_w], LANE),
                            _pad_cols(a[:, 3 * W + lora_w:3 * W + lora_w + lora_a], LANE),
                            a[:, 3 * W + lora_w + lora_a:]], axis=1)


def _rwkv_slots_to_cols(a, lora_w, lora_a):
    W = RWKV_WIDTH
    return jnp.concatenate([a[:, :3 * W], a[:, 3 * W:3 * W + lora_w],
                            a[:, 3 * W + LANE:3 * W + LANE + lora_a], a[:, 3 * W + 2 * LANE:]], axis=1)


def kernel(x_prompt, x_sample, mem_prompt, cache_fox_k, cache_fox_v, cache_fox_logf, page_table, state_rwkv, state_rwkv_shift, cache_mem_k, cache_mem_v, norm_mix, w_in, fox_q_gain, fox_k_gain, fox_b_f, rwkv_mu, rwkv_w0, rwkv_w2, rwkv_a0, rwkv_a2, rwkv_g2, rwkv_k_k, rwkv_k_a, rwkv_r_k, rwkv_gn_w, rwkv_gn_b, w_out, norm_mem_q, norm_mem_src, w_mem_q, w_mem_kv, mem_q_gain, mem_k_gain, w_mem_o, norm_ffn, w_router_group, w_router_expert, w_exp_gate, w_exp_up, w_exp_down):
    n_b, seq, d_model = x_prompt.shape
    n_dec, dec_seq, _ = x_sample.shape
    assert n_b == 1
    n_p = n_b * seq
    n_s = n_dec * dec_seq
    n_tok = n_p + n_s
    lora_w = rwkv_w2.shape[0]
    lora_a = rwkv_a2.shape[0]
    W = RWKV_WIDTH

    x = jnp.concatenate([x_prompt.reshape(n_p, d_model), x_sample.reshape(n_s, d_model)], axis=0)

    fox_w, rwkv_w = w_in[:, :3 * FOX_WIDTH + FOX_HEADS], w_in[:, 3 * FOX_WIDTH + FOX_HEADS:]
    w1 = jnp.concatenate([fox_w[:, :3 * FOX_WIDTH], _rwkv_cols_to_slots(rwkv_w, lora_w, lora_a),
                          _pad_cols(fox_w[:, 3 * FOX_WIDTH:], LANE)], axis=1)
    w1 = _pad_cols(w1, PROJ_COLS_PADDED).astype(BF16)
    head_gain = jnp.concatenate([jnp.tile(fox_q_gain, FOX_HEADS), jnp.tile(fox_k_gain, FOX_HEADS),
                                 jnp.ones((PROJ_COLS_PADDED - 2 * FOX_WIDTH,), F32)])
    proj = fused_matmul([x], [w1], norm_gain=norm_mix, head_gain=head_gain,
                        n_head_tiles=2 * FOX_WIDTH // PROJ_TN, tn=PROJ_TN, name="proj_in")

    logf_p, c_p = fox_logf_cumsum(proj, 0, n_p, fox_b_f)
    logf_s, _ = fox_logf_cumsum(proj, n_p, n_s, fox_b_f)
    fox_out_p = fox_prompt_attention(proj, c_p, seq)
    fox_out_s = fox_decode_attention(proj, n_p, logf_s, cache_fox_k, cache_fox_v, cache_fox_logf,
                                     page_table, dec_seq)
    fox_out = jnp.concatenate([fox_out_p, fox_out_s], axis=0)

    p_rwkv = proj[:, COL_RWKV:COL_RWKV + RWKV_COLS_PADDED]
    shift_s = _rwkv_cols_to_slots(state_rwkv_shift, lora_w, lora_a)
    prev_p = jnp.concatenate([jnp.zeros((1, RWKV_COLS_PADDED), F32), p_rwkv[:n_p - 1]], axis=0)
    p_s = p_rwkv[n_p:].reshape(n_dec, dec_seq, RWKV_COLS_PADDED)
    prev_s = jnp.concatenate([shift_s[:, None, :], p_s[:, :-1]], axis=1).reshape(n_s, -1)
    prev = jnp.concatenate([prev_p, prev_s], axis=0)
    mu = _rwkv_cols_to_slots(rwkv_mu.reshape(1, -1), lora_w, lora_a)
    r, dec, k2, v, a_vec, b_vec, g = rwkv_prep(
        proj, prev, mu, rwkv_w0, _pad_rows(rwkv_w2, LANE).astype(BF16), rwkv_a0,
        _pad_rows(rwkv_a2, LANE).astype(BF16), rwkv_g2.astype(BF16), rwkv_k_k, rwkv_k_a)
    seqs = (r, dec, k2, v, a_vec, b_vec)
    s0_p = jnp.zeros((n_b, RWKV_HEAD_DIM, W), F32)
    s0_s = state_rwkv.astype(F32).transpose(0, 2, 1, 3).reshape(n_dec, RWKV_HEAD_DIM, W)
    o_p, st_p = rwkv_scan(seqs, 0, n_b, seq, s0_p)
    o_s, st_s = rwkv_scan(seqs, n_p, n_dec, dec_seq, s0_s, n_par=8 if n_dec % 8 == 0 else 1,
                          unroll=1)
    o = jnp.concatenate([o_p, o_s], axis=0)
    rwkv_out = rwkv_post(o, r, k2, v, g, rwkv_r_k, rwkv_gn_w, rwkv_gn_b)
    unstate = lambda s: s.reshape(-1, RWKV_HEAD_DIM, RWKV_HEADS, RWKV_HEAD_DIM).transpose(0, 2, 1, 3)

    w_out_b = w_out.astype(BF16)
    h1 = fused_matmul([fox_out, rwkv_out], [w_out_b[:FOX_WIDTH], w_out_b[FOX_WIDTH:]], residual=x,
                      name="proj_out")

    mem_kv = fused_matmul([mem_prompt.reshape(-1, d_model)], [w_mem_kv.astype(BF16)],
                          norm_gain=norm_mem_src,
                          head_gain=jnp.concatenate([jnp.tile(mem_k_gain, MEM_HEADS),
                                                     jnp.ones((MEM_WIDTH,), F32)]),
                          n_head_tiles=1, tn=MEM_WIDTH, name="mem_kv")
    n_mem = mem_prompt.shape[1]
    mem_k_p = mem_kv[:, :MEM_WIDTH].reshape(n_b, n_mem, MEM_WIDTH)
    mem_v_p = mem_kv[:, MEM_WIDTH:].reshape(n_b, n_mem, MEM_WIDTH)
    q_mem = fused_matmul([h1], [w_mem_q.astype(BF16)], norm_gain=norm_mem_q,
                         head_gain=jnp.tile(mem_q_gain, MEM_HEADS), n_head_tiles=1, tn=MEM_WIDTH,
                         name="mem_q")
    att_p = memory_attention(q_mem, 0, n_b, seq, mem_k_p, mem_v_p)
    att_s = memory_attention(q_mem, n_p, n_dec, dec_seq,
                             cache_mem_k.reshape(n_dec, -1, MEM_WIDTH),
                             cache_mem_v.reshape(n_dec, -1, MEM_WIDTH))
    h2 = fused_matmul([jnp.concatenate([att_p, att_s], axis=0)], [w_mem_o.astype(BF16)],
                      residual=h1, name="mem_out")

    w_router = _pad_cols(jnp.concatenate([w_router_group, w_router_expert], axis=1), LANE)
    logits, xn = fused_matmul([h2], [w_router], norm_gain=norm_ffn, emit_lhs=True, exact=True,
                              tn=LANE, name="moe_logits")
    idx, wts = moe_router(logits)
    slot_tok, block_expert, slots, n_used, n_blocks = moe_dispatch_tables(idx[:, :TOP_K], n_tok)
    xn_pad = jnp.concatenate([xn, jnp.zeros((8, d_model), F32)], axis=0)
    yb = moe_experts(xn_pad, slot_tok, w_exp_gate.astype(BF16), w_exp_up.astype(BF16),
                     w_exp_down.astype(BF16), block_expert, n_used, n_blocks)
    y = moe_combine(yb, h2, wts, slots)

    heads = lambda t, b, s: t.reshape(b, s, FOX_HEADS, FOX_HEAD_DIM)
    p_rwkv_last_p = _rwkv_slots_to_cols(p_rwkv[n_p - 1:n_p], lora_w, lora_a)
    p_rwkv_last_s = _rwkv_slots_to_cols(p_s[:, -1], lora_w, lora_a)
    return (y[:n_p].reshape(n_b, seq, d_model), y[n_p:].reshape(n_dec, dec_seq, d_model),
            heads(proj[:n_p, COL_K:COL_K + FOX_WIDTH], n_b, seq),
            heads(proj[:n_p, COL_V:COL_V + FOX_WIDTH], n_b, seq),
            logf_p.reshape(n_b, seq, FOX_HEADS),
            unstate(st_p), p_rwkv_last_p,
            mem_k_p.reshape(n_b, n_mem, MEM_HEADS, MEM_HEAD_DIM),
            mem_v_p.reshape(n_b, n_mem, MEM_HEADS, MEM_HEAD_DIM),
            heads(proj[n_p:, COL_K:COL_K + FOX_WIDTH], n_dec, dec_seq),
            heads(proj[n_p:, COL_V:COL_V + FOX_WIDTH], n_dec, dec_seq),
            logf_s.reshape(n_dec, dec_seq, FOX_HEADS),
            unstate(st_s), p_rwkv_last_s)
```

```python
import functools

import numpy as np
import jax
import jax.numpy as jnp
from jax import lax
from jax.experimental import pallas as pl
from jax.experimental.pallas import tpu as pltpu

F32 = jnp.float32
BF16 = jnp.bfloat16
HI = lax.Precision.HIGHEST

LANE = 128
FOX_HEADS = 16
FOX_HEAD_DIM = 128
FOX_WIDTH = FOX_HEADS * FOX_HEAD_DIM
RWKV_HEADS = 32
RWKV_HEAD_DIM = 64
RWKV_WIDTH = RWKV_HEADS * RWKV_HEAD_DIM
MEM_HEADS = 4
MEM_HEAD_DIM = 128
MEM_WIDTH = MEM_HEADS * MEM_HEAD_DIM
N_GROUPS = 8
EXPERTS_PER_GROUP = 8
N_EXPERTS = N_GROUPS * EXPERTS_PER_GROUP
TOP_K = 2
EXPERT_BLOCK = 128
NORM_EPS = 1e-6
RWKV_GN_EPS = 64e-5
FOX_SCALE = FOX_HEAD_DIM ** -0.5
LOG2E = 1.4426950408889634
MEM_SCALE = MEM_HEAD_DIM ** -0.5
VMEM_LIMIT = 56 * 1024 * 1024

COL_Q = 0
COL_K = FOX_WIDTH
COL_V = 2 * FOX_WIDTH
COL_RWKV = 3 * FOX_WIDTH
LORA_SLOTS = 2 * LANE + 256
RWKV_COLS_PADDED = 3 * RWKV_WIDTH + LORA_SLOTS
COL_F = COL_RWKV + RWKV_COLS_PADDED
PROJ_COLS = COL_F + LANE
PROJ_TN = 512
PROJ_COLS_PADDED = -(-PROJ_COLS // PROJ_TN) * PROJ_TN


def _cparams(sem):
    return pltpu.CompilerParams(dimension_semantics=sem, vmem_limit_bytes=VMEM_LIMIT)


def _dot(a, b, precision=None):
    return jnp.dot(a, b, preferred_element_type=F32, precision=precision)


def _dot_nt(a, b, precision=None):
    return lax.dot_general(a, b, (((1,), (1,)), ((), ())), preferred_element_type=F32,
                           precision=precision)


def _iota(shape, dim):
    return lax.broadcasted_iota(jnp.int32, shape, dim)


def _mm_body(*refs, n_lhs, has_norm, has_res, n_head_tiles, emit_lhs, exact, tn):
    it = iter(refs)
    lhs = [next(it) for _ in range(n_lhs)]
    gain = next(it) if has_norm else None
    ws = [next(it) for _ in range(n_lhs)]
    res = next(it) if has_res else None
    hgain = next(it) if n_head_tiles else None
    out = next(it)
    lhs_out = next(it) if emit_lhs else None
    xs = [next(it) for _ in range(n_lhs)]
    j = pl.program_id(1)

    @pl.when(j == 0)
    def _():
        for l_ref, x_ref in zip(lhs, xs):
            v = l_ref[...]
            if has_norm:
                v = v * lax.rsqrt(jnp.mean(v * v, axis=-1, keepdims=True) + NORM_EPS) * gain[...]
            x_ref[...] = v.astype(x_ref.dtype)
            if emit_lhs:
                lhs_out[...] = v

    acc = None
    for x_ref, w_ref in zip(xs, ws):
        d = _dot(x_ref[...], w_ref[...], HI if exact else None)
        acc = d if acc is None else acc + d
    if has_res:
        acc = acc + res[...]
    if n_head_tiles:
        @pl.when(j < n_head_tiles)
        def _():
            g = hgain[...]
            for s in range(tn // LANE):
                seg = acc[:, s * LANE:(s + 1) * LANE]
                ms = jnp.mean(seg * seg, axis=-1, keepdims=True)
                out[:, s * LANE:(s + 1) * LANE] = (seg * lax.rsqrt(ms + NORM_EPS)
                                                   * g[:, s * LANE:(s + 1) * LANE])

        @pl.when(j >= n_head_tiles)
        def _():
            out[...] = acc
    else:
        out[...] = acc


def fused_matmul(lhs, ws, *, norm_gain=None, residual=None, head_gain=None, n_head_tiles=0,
                 emit_lhs=False, exact=False, tm=512, tn=512, name="fused_matmul"):
    n_lhs = len(lhs)
    m = lhs[0].shape[0]
    n = ws[0].shape[1]
    tm = min(tm, m)
    tn = min(tn, n)
    assert m % tm == 0 and n % tn == 0
    has_norm = norm_gain is not None
    has_res = residual is not None
    assert not (has_norm and n_lhs != 1)
    in_specs, args = [], []
    for l in lhs:
        in_specs.append(pl.BlockSpec((tm, l.shape[1]), lambda i, j: (i, 0)))
        args.append(l)
    if has_norm:
        in_specs.append(pl.BlockSpec((1, lhs[0].shape[1]), lambda i, j: (0, 0)))
        args.append(norm_gain.reshape(1, -1).astype(F32))
    for w in ws:
        in_specs.append(pl.BlockSpec((w.shape[0], tn), lambda i, j: (0, j)))
        args.append(w)
    if has_res:
        in_specs.append(pl.BlockSpec((tm, tn), lambda i, j: (i, j)))
        args.append(residual)
    if n_head_tiles:
        in_specs.append(pl.BlockSpec((1, tn), lambda i, j: (0, j)))
        args.append(head_gain.reshape(1, -1).astype(F32))
    out_shape = [jax.ShapeDtypeStruct((m, n), F32)]
    out_specs = [pl.BlockSpec((tm, tn), lambda i, j: (i, j))]
    if emit_lhs:
        out_shape.append(jax.ShapeDtypeStruct((m, lhs[0].shape[1]), F32))
        out_specs.append(pl.BlockSpec((tm, lhs[0].shape[1]), lambda i, j: (i, 0)))
    scratch = [pltpu.VMEM((tm, l.shape[1]), F32 if exact else BF16) for l in lhs]
    body = functools.partial(_mm_body, n_lhs=n_lhs, has_norm=has_norm, has_res=has_res,
                             n_head_tiles=n_head_tiles, emit_lhs=emit_lhs, exact=exact, tn=tn)
    res = pl.pallas_call(
        body, grid=(m // tm, n // tn), in_specs=in_specs, out_specs=out_specs,
        out_shape=out_shape, scratch_shapes=scratch,
        compiler_params=_cparams(("parallel", "arbitrary")), name=name)(*args)
    return res if emit_lhs else res[0]


def _log_sigmoid(x):
    return jnp.minimum(x, 0.0) - jnp.log1p(jnp.exp(-jnp.abs(x)))


def _logf_body(f_ref, b_ref, logf_ref, c_ref, carry, *, tc):
    i = pl.program_id(0)

    @pl.when(i == 0)
    def _():
        carry[...] = jnp.zeros_like(carry)

    lf = _log_sigmoid(f_ref[:, :FOX_HEADS] + b_ref[...])
    logf_ref[...] = lf
    tri = (_iota((tc, tc), 0) >= _iota((tc, tc), 1)).astype(F32)
    c = _dot(tri, lf, HI) + carry[...]
    c_ref[...] = c
    carry[...] = c[tc - 1:tc, :]


def fox_logf_cumsum(proj, row0, rows, b_f, *, tc=512):
    tc = min(tc, rows)
    assert rows % tc == 0 and row0 % tc == 0
    rb0 = row0 // tc
    return pl.pallas_call(
        functools.partial(_logf_body, tc=tc), grid=(rows // tc,),
        in_specs=[pl.BlockSpec((tc, LANE), lambda i: (rb0 + i, COL_F // LANE)),
                  pl.BlockSpec((1, FOX_HEADS), lambda i: (0, 0))],
        out_specs=[pl.BlockSpec((tc, FOX_HEADS), lambda i: (i, 0)),
                   pl.BlockSpec((tc, FOX_HEADS), lambda i: (i, 0))],
        out_shape=[jax.ShapeDtypeStruct((rows, FOX_HEADS), F32)] * 2,
        scratch_shapes=[pltpu.VMEM((1, FOX_HEADS), F32)],
        compiler_params=_cparams(("arbitrary",)), name="fox_logf_cumsum")(
            proj, b_f.reshape(1, FOX_HEADS).astype(F32))


def _flash_body(qi_tbl, ki_tbl, q_ref, k_ref, v_ref, c_ref, ct_ref, o_ref, m_s, l_s, acc_s, cq_s,
                *, tq, tk):
    h = pl.program_id(0)
    s_idx = pl.program_id(1)
    qi = qi_tbl[s_idx]
    ki = ki_tbl[s_idx]

    @pl.when(ki == 0)
    def _():
        m_s[...] = jnp.full_like(m_s, -jnp.inf)
        l_s[...] = jnp.zeros_like(l_s)
        acc_s[...] = jnp.zeros_like(acc_s)
        lane = _iota((tq, FOX_HEADS), 1)
        cq_s[...] = jnp.sum(jnp.where(lane == h, c_ref[...], 0.0), axis=1, keepdims=True)

    def update(masked):
        q = (q_ref[...] * (FOX_SCALE * LOG2E)).astype(BF16)
        s = _dot_nt(q, k_ref[...].astype(BF16)) - ct_ref[...] * LOG2E
        if masked:
            q_pos = qi * tq + _iota((tq, tk), 0)
            k_pos = ki * tk + _iota((tq, tk), 1)
            s = jnp.where(k_pos <= q_pos, s, -jnp.inf)
        cq = cq_s[...] * LOG2E
        m_prev = m_s[...]
        m_new = jnp.maximum(m_prev, jnp.max(s, axis=1, keepdims=True) + cq)
        alpha = jnp.exp2(m_prev - m_new)
        p = jnp.exp2(s - (m_new - cq))
        l_s[...] = alpha * l_s[...] + jnp.sum(p, axis=1, keepdims=True)
        acc_s[...] = alpha * acc_s[...] + _dot(p.astype(BF16), v_ref[...].astype(BF16))
        m_s[...] = m_new

    crosses = (ki + 1) * tk - 1 > qi * tq

    @pl.when(crosses)
    def _():
        update(True)

    @pl.when(jnp.logical_not(crosses))
    def _():
        update(False)

    @pl.when(ki == (qi + 1) * (tq // tk) - 1)
    def _():
        o_ref[...] = acc_s[...] / l_s[...]


def fox_prompt_attention(proj, c, seq, *, tq=1024, tk=1024):
    tq = min(tq, seq)
    tk = min(tk, tq)
    assert seq % tq == 0 and tq % tk == 0
    nq = seq // tq
    qi_np, ki_np = [], []
    for qi in range(nq):
        for ki in range((qi + 1) * (tq // tk)):
            qi_np.append(qi)
            ki_np.append(ki)
    qi_tbl = jnp.asarray(np.array(qi_np, np.int32))
    ki_tbl = jnp.asarray(np.array(ki_np, np.int32))
    ct = c.T.reshape(FOX_HEADS, 1, seq)
    grid_spec = pltpu.PrefetchScalarGridSpec(
        num_scalar_prefetch=2, grid=(FOX_HEADS, len(qi_np)),
        in_specs=[
            pl.BlockSpec((tq, FOX_HEAD_DIM), lambda h, s, qt, kt: (qt[s], COL_Q // LANE + h)),
            pl.BlockSpec((tk, FOX_HEAD_DIM), lambda h, s, qt, kt: (kt[s], COL_K // LANE + h)),
            pl.BlockSpec((tk, FOX_HEAD_DIM), lambda h, s, qt, kt: (kt[s], COL_V // LANE + h)),
            pl.BlockSpec((tq, FOX_HEADS), lambda h, s, qt, kt: (qt[s], 0)),
            pl.BlockSpec((None, 1, tk), lambda h, s, qt, kt: (h, 0, kt[s])),
        ],
        out_specs=pl.BlockSpec((tq, FOX_HEAD_DIM), lambda h, s, qt, kt: (qt[s], h)),
        scratch_shapes=[pltpu.VMEM((tq, 1), F32), pltpu.VMEM((tq, 1), F32),
                        pltpu.VMEM((tq, FOX_HEAD_DIM), F32), pltpu.VMEM((tq, 1), F32)])
    return pl.pallas_call(
        functools.partial(_flash_body, tq=tq, tk=tk), grid_spec=grid_spec,
        out_shape=jax.ShapeDtypeStruct((seq, FOX_WIDTH), F32),
        compiler_params=_cparams(("parallel", "arbitrary")), name="fox_prompt_attention")(
            qi_tbl, ki_tbl, proj, proj, proj, c, ct)


def _page_bias_body(lf_ref, sfx_ref, tot_ref, *, page):
    H = FOX_HEADS
    lf = lf_ref[...]
    n_lane = lf.shape[1]
    lane = _iota(lf.shape, 1)

    def later(x, sh):
        return jnp.where(lane < n_lane - sh, pltpu.roll(x, n_lane - sh, 1), 0.0)

    total = lf
    sfx = later(lf, H)
    for kk in range(page.bit_length() - 1):
        total = total + pltpu.roll(total, H << kk, 1)
        sfx = sfx + later(sfx, H << kk)
    sfx_ref[...] = sfx
    tot_ref[...] = total


def fox_page_bias(cache_logf):
    n_pool, page, _ = cache_logf.shape
    assert page & (page - 1) == 0
    lf = cache_logf.reshape(n_pool, page * FOX_HEADS)
    tp = max(t for t in range(8, min(n_pool, 256) + 1, 8) if n_pool % t == 0)
    blk = pl.BlockSpec((tp, page * FOX_HEADS), lambda i: (i, 0))
    return pl.pallas_call(
        functools.partial(_page_bias_body, page=page), grid=(n_pool // tp,), in_specs=[blk],
        out_specs=[blk, blk], out_shape=[jax.ShapeDtypeStruct(lf.shape, F32)] * 2,
        compiler_params=_cparams(("parallel",)), name="fox_page_bias")(lf)


def _decode_body(pt_ref, q_ref, kn_ref, vn_ref, lfn_ref, mask_ref, *refs,
                 n_steps, pages_per_step, tnew):
    page_refs = refs[:4 * pages_per_step]
    o_ref, q_s, m_s, l_s, acc_s, carry_s, ccol_s = refs[4 * pages_per_step:]
    p = pl.program_id(1)
    H = FOX_HEADS
    D = FOX_HEAD_DIM
    rows = H * tnew

    def head_rows(x):
        return jnp.concatenate([x[:, h * D:(h + 1) * D] for h in range(H)], axis=0)

    @pl.when(p == 0)
    def _():
        q2 = head_rows(q_ref[...] * FOX_SCALE).astype(BF16)
        q_s[...] = q2
        kn = head_rows(kn_ref[...]).astype(BF16)
        vn = head_rows(vn_ref[...]).astype(BF16)
        spread = (_iota((H, rows), 0) == _iota((H, rows), 1) // tnew).astype(F32)
        lf_exp = _dot(lfn_ref[...], spread, HI)
        upto = _iota((tnew, rows), 0) <= _iota((tnew, rows), 1) % tnew
        cn_row = jnp.sum(jnp.where(upto, lf_exp, 0.0), axis=0, keepdims=True)
        ri = _iota((rows, rows), 0)
        li = _iota((rows, rows), 1)
        ccol = jnp.sum(jnp.where(ri == li, cn_row, 0.0), axis=1, keepdims=True)
        ccol_s[...] = ccol
        s = _dot_nt(q2, kn) + ccol - cn_row
        valid = jnp.logical_and(ri // tnew == li // tnew, li % tnew <= ri % tnew)
        s = jnp.where(valid, s, -jnp.inf)
        m = jnp.max(s, axis=1, keepdims=True)
        pr = jnp.exp(s - m)
        m_s[...] = m
        l_s[...] = jnp.sum(pr, axis=1, keepdims=True)
        acc_s[...] = _dot(pr.astype(BF16), vn)
        carry_s[...] = jnp.zeros_like(carry_s)

    ccol = ccol_s[...]
    for j in range(pages_per_step):
        kc_ref, vc_ref, sfx_ref, tot_ref = page_refs[4 * j:4 * j + 4]
        bias = sfx_ref[...] + carry_s[...]
        s = _dot_nt(q_s[...], kc_ref[...].astype(BF16)) + (mask_ref[...] + bias)
        m_prev = m_s[...]
        m_new = jnp.maximum(m_prev, jnp.max(s, axis=1, keepdims=True) + ccol)
        alpha = jnp.exp(m_prev - m_new)
        pr = jnp.exp(s - (m_new - ccol))
        l_s[...] = alpha * l_s[...] + jnp.sum(pr, axis=1, keepdims=True)
        acc_s[...] = alpha * acc_s[...] + _dot(pr.astype(BF16), vc_ref[...].astype(BF16))
        m_s[...] = m_new
        carry_s[...] = carry_s[...] + tot_ref[...]

    @pl.when(p == n_steps - 1)
    def _():
        o = acc_s[...] / l_s[...]
        o_ref[...] = jnp.concatenate([o[h * tnew:(h + 1) * tnew, :] for h in range(H)], axis=1)


def fox_decode_attention(proj, row0, logf_new, cache_k, cache_v, cache_logf, page_table, tnew):
    n_seq, n_pages = page_table.shape
    n_pool, page = cache_k.shape[0], cache_k.shape[1]
    assert row0 % tnew == 0 and tnew == 8 and page & (page - 1) == 0
    rb0 = row0 // tnew
    kc = cache_k.reshape(n_pool, page * FOX_HEADS, FOX_HEAD_DIM)
    vc = cache_v.reshape(n_pool, page * FOX_HEADS, FOX_HEAD_DIM)
    n_lane = page * FOX_HEADS
    sfx, tot = fox_page_bias(cache_logf)
    sfx = sfx.reshape(n_pool, 1, n_lane)
    tot = tot.reshape(n_pool, 1, n_lane)
    rows = FOX_HEADS * tnew
    keep = (np.arange(n_lane)[None, :] % FOX_HEADS) == (np.arange(rows)[:, None] // tnew)
    mask = jnp.asarray(np.where(keep, 0.0, -np.inf).astype(np.float32))
    pps = 2 if n_pages % 2 == 0 else 1
    n_steps = n_pages // pps

    def paged(j):
        return lambda b, p, pt: (pt[b, n_pages - 1 - (p * pps + j)], 0, 0)

    page_specs, page_args = [], []
    for j in range(pps):
        page_specs += [pl.BlockSpec((None, n_lane, FOX_HEAD_DIM), paged(j)),
                       pl.BlockSpec((None, n_lane, FOX_HEAD_DIM), paged(j)),
                       pl.BlockSpec((None, 1, n_lane), paged(j)),
                       pl.BlockSpec((None, 1, n_lane), paged(j))]
        page_args += [kc, vc, sfx, tot]
    grid_spec = pltpu.PrefetchScalarGridSpec(
        num_scalar_prefetch=1, grid=(n_seq, n_steps),
        in_specs=[
            pl.BlockSpec((tnew, FOX_WIDTH), lambda b, p, pt: (rb0 + b, COL_Q // FOX_WIDTH)),
            pl.BlockSpec((tnew, FOX_WIDTH), lambda b, p, pt: (rb0 + b, COL_K // FOX_WIDTH)),
            pl.BlockSpec((tnew, FOX_WIDTH), lambda b, p, pt: (rb0 + b, COL_V // FOX_WIDTH)),
            pl.BlockSpec((tnew, FOX_HEADS), lambda b, p, pt: (b, 0)),
            pl.BlockSpec((rows, n_lane), lambda b, p, pt: (0, 0)),
        ] + page_specs,
        out_specs=pl.BlockSpec((tnew, FOX_WIDTH), lambda b, p, pt: (b, 0)),
        scratch_shapes=[pltpu.VMEM((rows, FOX_HEAD_DIM), BF16), pltpu.VMEM((rows, 1), F32),
                        pltpu.VMEM((rows, 1), F32), pltpu.VMEM((rows, FOX_HEAD_DIM), F32),
                        pltpu.VMEM((1, n_lane), F32), pltpu.VMEM((rows, 1), F32)])
    return pl.pallas_call(
        functools.partial(_decode_body, n_steps=n_steps, pages_per_step=pps, tnew=tnew),
        grid_spec=grid_spec, out_shape=jax.ShapeDtypeStruct((n_seq * tnew, FOX_WIDTH), F32),
        compiler_params=_cparams(("parallel", "arbitrary")), name="fox_decode_attention")(
            page_table, proj, proj, proj, logf_new, mask, *page_args)


def _head_sum_matrix(width):
    return (_iota((width, width), 0) // RWKV_HEAD_DIM
            == _iota((width, width), 1) // RWKV_HEAD_DIM)


def _head_sum_exact(x):
    pm = _head_sum_matrix(LANE).astype(F32)
    return jnp.concatenate([_dot(x[:, j * LANE:(j + 1) * LANE], pm, HI)
                            for j in range(x.shape[1] // LANE)], axis=1)


def _softplus(y):
    return jnp.maximum(y, 0.0) + jnp.log1p(jnp.exp(-jnp.abs(y)))


def _rwkv_prep_body(pr_ref, pk_ref, pv_ref, pl_ref, er_ref, ek_ref, ev_ref, el_ref,
                    fr_ref, fk_ref, fv_ref, fl_ref, mr_ref, mk_ref, mv_ref, ml_ref,
                    w0_ref, w2_ref, a0_ref, a2_ref, g2_ref, kk_ref, ka_ref,
                    r_o, w_o, k_o, v_o, a_o, b_o, g_o, *, tm, n_long_tiles, short_len):
    i = pl.program_id(0)
    is_short = i >= n_long_tiles

    def shift(p_ref, e_ref, f_ref, m_ref):
        p = p_ref[...]
        rolled = pltpu.roll(p, 1, 0)
        row = _iota(p.shape, 0)
        before_tile = jnp.where(i == 0, 0.0, e_ref[7:8, :])
        long_prev = jnp.where(row == 0, before_tile, rolled)
        short_prev = jnp.where(row % short_len == 0, f_ref[...], rolled)
        prev = jnp.where(is_short, short_prev, long_prev)
        return p + m_ref[...] * (prev - p)

    r = shift(pr_ref, er_ref, fr_ref, mr_ref)
    k = shift(pk_ref, ek_ref, fk_ref, mk_ref)
    v = shift(pv_ref, ev_ref, fv_ref, mv_ref)
    zl = shift(pl_ref, el_ref, fl_ref, ml_ref)
    xw = zl[:, 0:LANE]
    xa = zl[:, LANE:2 * LANE]
    xg = zl[:, 2 * LANE:]
    lw = w0_ref[...] + _dot(jnp.tanh(xw).astype(BF16), w2_ref[...])
    w_log = -_softplus(-lw) - 0.5
    a = jax.nn.sigmoid(a0_ref[...] + _dot(xa.astype(BF16), a2_ref[...]))
    kk = k * kk_ref[...]
    kk = kk * lax.rsqrt(jnp.maximum(_head_sum_exact(kk * kk), 1e-24))
    r_o[...] = r
    w_o[...] = jnp.exp(-jnp.exp(w_log))
    k_o[...] = k * (1.0 + (a - 1.0) * ka_ref[...])
    v_o[...] = v
    a_o[...] = -kk
    b_o[...] = kk * a
    g_o[...] = _dot(jax.nn.sigmoid(xg).astype(BF16), g2_ref[...])


def rwkv_prep(proj, n_long, short_first, short_len, mu, w0, w2, a0, a2, g2, k_k, k_a, *, tm=128):
    m = proj.shape[0]
    tm = min(tm, m)
    W = RWKV_WIDTH
    assert m % tm == 0 and n_long % tm == 0 and tm % short_len == 0 and tm % 8 == 0
    assert COL_RWKV % W == 0 and (COL_RWKV + 3 * W) % LORA_SLOTS == 0
    n_long_tiles = n_long // tm
    row = lambda a: a.reshape(1, -1).astype(F32)
    vec = pl.BlockSpec((1, W), lambda i: (0, 0))

    def pieces(rows, col0, row_block):
        specs = [pl.BlockSpec((rows, W), functools.partial(lambda i, c: (row_block(i), c),
                                                           c=col0 // W + n)) for n in range(3)]
        specs.append(pl.BlockSpec((rows, LORA_SLOTS),
                                  lambda i: (row_block(i), (col0 + 3 * W) // LORA_SLOTS)))
        return specs

    body = functools.partial(_rwkv_prep_body, tm=tm, n_long_tiles=n_long_tiles,
                             short_len=short_len)
    return pl.pallas_call(
        body, grid=(m // tm,),
        in_specs=(pieces(tm, COL_RWKV, lambda i: i)
                  + pieces(8, COL_RWKV, lambda i: jnp.maximum(i * (tm // 8) - 1, 0))
                  + pieces(tm, 0, lambda i: jnp.maximum(i - n_long_tiles, 0))
                  + pieces(1, 0, lambda i: 0)
                  + [vec, pl.BlockSpec((LANE, W), lambda i: (0, 0)), vec,
                     pl.BlockSpec((LANE, W), lambda i: (0, 0)),
                     pl.BlockSpec((256, W), lambda i: (0, 0)), vec, vec]),
        out_specs=[pl.BlockSpec((tm, W), lambda i: (i, 0))] * 7,
        out_shape=[jax.ShapeDtypeStruct((m, W), F32)] * 7,
        compiler_params=_cparams(("parallel",)), name="rwkv_prep")(
            *([proj] * 8 + [short_first] * 4 + [mu] * 4),
            row(w0), w2, row(a0), a2, g2, row(k_k), row(k_a))


def _scan_body(r_ref, w_ref, k_ref, v_ref, a_ref, b_ref, s0_ref, o_ref, st_ref, state, *, tc,
               n_chunks, n_par, unroll):
    c = pl.program_id(1)
    N = RWKV_HEAD_DIM
    W = RWKV_WIDTH
    SEG = 256
    n_seg = W // SEG

    @pl.when(c == 0)
    def _():
        state[...] = s0_ref[...]

    diag = (_iota((N, W), 1) % N == _iota((N, W), 0)).astype(F32)
    pm = _head_sum_matrix(SEG).astype(BF16)

    def head_sums(parts):
        n = len(parts)
        tiles = [p[:, j * SEG:(j + 1) * SEG] for j in range(n_seg) for p in parts]
        y = _dot(jnp.concatenate(tiles, axis=0), pm)
        return [jnp.concatenate([y[(j * n + i) * N:(j * n + i + 1) * N, :]
                                 for j in range(n_seg)], axis=1) for i in range(n)]

    def row(ref, g, t):
        return ref[pl.ds(g * tc + t, 1), :]

    def out_row(o_bc):
        return jnp.sum(o_bc * diag, axis=0, keepdims=True)

    def advance(t, with_prev):
        parts, n_each = [], 3 if with_prev else 2
        for g in range(n_par):
            s = state[g]
            parts += [(s * row(a_ref, g, t)).astype(BF16), (row(v_ref, g, t) * diag).astype(BF16)]
            if with_prev:
                parts.append((s * row(r_ref, g, t - 1)).astype(BF16))
        sums = head_sums(parts)
        for g in range(n_par):
            sa, vc = sums[g * n_each:g * n_each + 2]
            state[g] = (state[g] * row(w_ref, g, t) + sa * row(b_ref, g, t)
                        + vc * row(k_ref, g, t))
            if with_prev:
                o_ref[pl.ds(g * tc + t - 1, 1), :] = out_row(sums[g * n_each + 2])

    advance(0, False)

    def step(t, carry):
        advance(t, True)
        return carry

    lax.fori_loop(1, tc, step, 0, unroll=unroll)
    last = head_sums([(state[g] * row(r_ref, g, tc - 1)).astype(BF16) for g in range(n_par)])
    for g in range(n_par):
        o_ref[pl.ds(g * tc + tc - 1, 1), :] = out_row(last[g])

    @pl.when(c == n_chunks - 1)
    def _():
        st_ref[...] = state[...]


def rwkv_scan(seqs, row0, n_seq, seq_len, s0, *, tc=256, n_par=1, unroll=2):
    tc = min(tc, seq_len)
    n_chunks = seq_len // tc
    assert seq_len % tc == 0 and row0 % (n_par * tc) == 0 and n_seq % n_par == 0
    assert n_par == 1 or n_chunks == 1
    rb0 = row0 // (n_par * tc)
    W = RWKV_WIDTH
    tok = pl.BlockSpec((n_par * tc, W), lambda s, c: (rb0 + s * n_chunks + c, 0))
    st = pl.BlockSpec((n_par, RWKV_HEAD_DIM, W), lambda s, c: (s, 0, 0))
    return pl.pallas_call(
        functools.partial(_scan_body, tc=tc, n_chunks=n_chunks, n_par=n_par,
                          unroll=min(unroll, tc - 1)),
        grid=(n_seq // n_par, n_chunks),
        in_specs=[tok] * 6 + [st],
        out_specs=[pl.BlockSpec((n_par * tc, W), lambda s, c: (s * n_chunks + c, 0)), st],
        out_shape=[jax.ShapeDtypeStruct((n_seq * seq_len, W), F32),
                   jax.ShapeDtypeStruct((n_seq, RWKV_HEAD_DIM, W), F32)],
        scratch_shapes=[pltpu.VMEM((n_par, RWKV_HEAD_DIM, W), F32)],
        compiler_params=_cparams(("parallel", "arbitrary")), name="rwkv_scan")(*seqs, s0)


def _rwkv_post_body(o_ref, r_ref, k_ref, v_ref, g_ref, rk_ref, gw_ref, gb_ref, out_ref):
    o = o_ref[...]
    inv_n = 1.0 / RWKV_HEAD_DIM
    mean = _head_sum_exact(o) * inv_n
    d = o - mean
    var = _head_sum_exact(d * d) * inv_n
    y = d * lax.rsqrt(var + RWKV_GN_EPS) * gw_ref[...] + gb_ref[...]
    y = y + _head_sum_exact(r_ref[...] * k_ref[...] * rk_ref[...]) * v_ref[...]
    out_ref[...] = y * g_ref[...]


def rwkv_post(o, r, k, v, g, r_k, gn_w, gn_b, *, tm=256):
    m = o.shape[0]
    tm = min(tm, m)
    assert m % tm == 0
    W = RWKV_WIDTH
    row = lambda a: a.reshape(1, -1).astype(F32)
    tok = pl.BlockSpec((tm, W), lambda i: (i, 0))
    vec = pl.BlockSpec((1, W), lambda i: (0, 0))
    return pl.pallas_call(
        _rwkv_post_body, grid=(m // tm,), in_specs=[tok] * 5 + [vec] * 3, out_specs=tok,
        out_shape=jax.ShapeDtypeStruct((m, W), F32),
        compiler_params=_cparams(("parallel",)), name="rwkv_post")(
            o, r, k, v, g, row(r_k), row(gn_w), row(gn_b))


def _memattn_body(q_ref, k_ref, v_ref, o_ref):
    q = q_ref[...]
    k = k_ref[...]
    v = v_ref[...]
    outs = []
    for h in range(MEM_HEADS):
        sl = slice(h * MEM_HEAD_DIM, (h + 1) * MEM_HEAD_DIM)
        s = _dot_nt(q[:, sl].astype(BF16), k[:, sl].astype(BF16)) * MEM_SCALE
        e = jnp.exp(s - jnp.max(s, axis=1, keepdims=True))
        pr = e / jnp.sum(e, axis=1, keepdims=True)
        outs.append(_dot(pr.astype(BF16), v[:, sl].astype(BF16)))
    o_ref[...] = jnp.concatenate(outs, axis=1)


def memory_attention(q, row0, n_seq, seq_len, mem_k, mem_v, *, tq=512):
    tq = min(tq, seq_len)
    assert seq_len % tq == 0 and row0 % tq == 0
    nt = seq_len // tq
    rb0 = row0 // tq
    m_tok = mem_k.shape[1]
    kv = pl.BlockSpec((None, m_tok, MEM_WIDTH), lambda b, t: (b, 0, 0))
    return pl.pallas_call(
        _memattn_body, grid=(n_seq, nt),
        in_specs=[pl.BlockSpec((tq, MEM_WIDTH), lambda b, t: (rb0 + b * nt + t, 0)), kv, kv],
        out_specs=pl.BlockSpec((tq, MEM_WIDTH), lambda b, t: (b * nt + t, 0)),
        out_shape=jax.ShapeDtypeStruct((n_seq * seq_len, MEM_WIDTH), F32),
        compiler_params=_cparams(("parallel", "parallel")), name="memory_attention")(
            q, mem_k, mem_v)


def _router_body(lg_ref, idx_ref, wt_ref):
    lg = lg_ref[...]
    shape = lg.shape
    lane = _iota(shape, 1).astype(F32)
    neg = -jnp.inf
    big = float(LANE)
    is_g = lane < N_GROUPS
    gl = jnp.where(is_g, lg, neg)
    gmax = jnp.max(gl, axis=1, keepdims=True)
    gsel = jnp.min(jnp.where(gl == gmax, lane, big), axis=1, keepdims=True)
    gsum = jnp.sum(jnp.where(is_g, jnp.exp(lg - gmax), 0.0), axis=1, keepdims=True)
    g_w = 1.0 / gsum
    base = N_GROUPS + gsel * EXPERTS_PER_GROUP
    in_g = jnp.logical_and(lane >= base, lane < base + EXPERTS_PER_GROUP)
    el = jnp.where(in_g, lg, neg)
    emax = jnp.max(el, axis=1, keepdims=True)
    ee = jnp.where(in_g, jnp.exp(lg - emax), 0.0)
    prob = ee / jnp.sum(ee, axis=1, keepdims=True)
    pm = jnp.where(in_g, prob, -1.0)
    p1 = jnp.max(pm, axis=1, keepdims=True)
    i1 = jnp.min(jnp.where(pm == p1, lane, big), axis=1, keepdims=True)
    pm2 = jnp.where(lane == i1, -1.0, pm)
    p2 = jnp.max(pm2, axis=1, keepdims=True)
    i2 = jnp.min(jnp.where(pm2 == p2, lane, big), axis=1, keepdims=True)
    tot = p1 + p2
    idx_ref[...] = jnp.where(lane == 0, i1 - N_GROUPS,
                             jnp.where(lane == 1, i2 - N_GROUPS, 0.0)).astype(jnp.int32)
    wt_ref[...] = jnp.where(lane == 0, g_w * (p1 / tot), jnp.where(lane == 1, g_w * (p2 / tot), 0.0))


def moe_router(logits, *, tm=512):
    m = logits.shape[0]
    tm = min(tm, m)
    assert m % tm == 0
    blk = pl.BlockSpec((tm, LANE), lambda i: (i, 0))
    return pl.pallas_call(
        _router_body, grid=(m // tm,), in_specs=[blk], out_specs=[blk, blk],
        out_shape=[jax.ShapeDtypeStruct((m, LANE), jnp.int32), jax.ShapeDtypeStruct((m, LANE), F32)],
        compiler_params=_cparams(("parallel",)), name="moe_router")(logits)


def _experts_body(be_ref, nused_ref, tok_ref, x_hbm, wg_ref, wu_ref, wd_ref, y_ref, xbuf, sem):
    i = pl.program_id(0)
    n_used = nused_ref[0]

    def gather(blk, buf):
        base = blk * EXPERT_BLOCK

        def issue(r, carry):
            pltpu.make_async_copy(x_hbm.at[pl.ds(tok_ref[base + r], 1)],
                                  xbuf.at[buf, pl.ds(r, 1)], sem.at[buf]).start()
            return carry

        lax.fori_loop(0, EXPERT_BLOCK, issue, 0)

    @pl.when(i == 0)
    def _():
        gather(0, 0)

    @pl.when(i < n_used)
    def _():
        buf = i % 2

        @pl.when(i + 1 < n_used)
        def _():
            gather(i + 1, 1 - buf)

        pltpu.make_async_copy(x_hbm.at[pl.ds(0, EXPERT_BLOCK)], xbuf.at[buf], sem.at[buf]).wait()
        x = xbuf[buf].astype(BF16)
        hb = jax.nn.silu(_dot(x, wg_ref[...])) * _dot(x, wu_ref[...])
        y_ref[...] = _dot(hb.astype(BF16), wd_ref[...])


def moe_experts(x_pad, slot_tok, wg, wu, wd, block_expert, n_used, n_blocks):
    d = x_pad.shape[1]
    de = wg.shape[2]
    rows = lambda i, be, nu, tk: (jnp.minimum(i, nu[0] - 1), 0)
    expert = lambda i, be, nu, tk: (be[i], 0, 0)
    grid_spec = pltpu.PrefetchScalarGridSpec(
        num_scalar_prefetch=3, grid=(n_blocks,),
        in_specs=[pl.BlockSpec(memory_space=pl.ANY),
                  pl.BlockSpec((None, d, de), expert),
                  pl.BlockSpec((None, d, de), expert),
                  pl.BlockSpec((None, de, d), expert)],
        out_specs=pl.BlockSpec((EXPERT_BLOCK, d), rows),
        scratch_shapes=[pltpu.VMEM((2, EXPERT_BLOCK, d), F32), pltpu.SemaphoreType.DMA((2,))])
    return pl.pallas_call(
        _experts_body, grid_spec=grid_spec,
        out_shape=jax.ShapeDtypeStruct((n_blocks * EXPERT_BLOCK, d), F32),
        compiler_params=_cparams(("arbitrary",)), name="moe_experts")(
            block_expert, n_used, slot_tok, x_pad, wg, wu, wd)


def _combine_body(slot_ref, yb_hbm, h_ref, w_ref, y_ref, g_s, sem, *, tm):
    i = pl.program_id(0)

    def issue(r, carry):
        tok = i * tm + r
        for kk in range(TOP_K):
            pltpu.make_async_copy(yb_hbm.at[pl.ds(slot_ref[TOP_K * tok + kk], 1)],
                                  g_s.at[kk, pl.ds(r, 1)], sem).start()
        return carry

    lax.fori_loop(0, tm, issue, 0)
    for kk in range(TOP_K):
        pltpu.make_async_copy(yb_hbm.at[pl.ds(0, tm)], g_s.at[kk], sem).wait()
    w = w_ref[...]
    y_ref[...] = h_ref[...] + (g_s[0] * w[:, 0:1] + g_s[1] * w[:, 1:2])


def moe_combine(yb, h, wts, slots, *, tm=128):
    m, d = h.shape
    tm = min(tm, m)
    assert m % tm == 0
    grid_spec = pltpu.PrefetchScalarGridSpec(
        num_scalar_prefetch=1, grid=(m // tm,),
        in_specs=[pl.BlockSpec(memory_space=pl.ANY),
                  pl.BlockSpec((tm, d), lambda i, s: (i, 0)),
                  pl.BlockSpec((tm, LANE), lambda i, s: (i, 0))],
        out_specs=pl.BlockSpec((tm, d), lambda i, s: (i, 0)),
        scratch_shapes=[pltpu.VMEM((TOP_K, tm, d), F32), pltpu.SemaphoreType.DMA(())])
    return pl.pallas_call(
        functools.partial(_combine_body, tm=tm), grid_spec=grid_spec,
        out_shape=jax.ShapeDtypeStruct((m, d), F32),
        compiler_params=_cparams(("arbitrary",)), name="moe_combine")(slots, yb, h, wts)


def moe_dispatch_tables(expert_idx, n_tok):
    n_assign = n_tok * TOP_K
    flat_e = expert_idx.reshape(n_assign).astype(jnp.int32)
    flat_tok = jnp.repeat(jnp.arange(n_tok, dtype=jnp.int32), TOP_K)
    order = jnp.argsort(flat_e)
    e_sorted = flat_e[order]
    counts = jnp.zeros((N_EXPERTS,), jnp.int32).at[flat_e].add(1)
    padded = (counts + EXPERT_BLOCK - 1) // EXPERT_BLOCK * EXPERT_BLOCK
    pad_end = jnp.cumsum(padded)
    pad_start = pad_end - padded
    start = jnp.cumsum(counts) - counts
    dest = pad_start[e_sorted] + jnp.arange(n_assign, dtype=jnp.int32) - start[e_sorted]
    n_blocks = -(-n_assign // EXPERT_BLOCK) + N_EXPERTS
    n_slots = n_blocks * EXPERT_BLOCK
    slot_tok = jnp.full((n_slots,), n_tok, jnp.int32).at[dest].set(flat_tok[order])
    block_start = jnp.arange(n_blocks, dtype=jnp.int32) * EXPERT_BLOCK
    block_expert = jnp.minimum(jnp.searchsorted(pad_end, block_start, side='right'),
                               N_EXPERTS - 1).astype(jnp.int32)
    slots = jnp.zeros((n_assign,), jnp.int32).at[order].set(dest)
    n_used = (pad_end[-1:] // EXPERT_BLOCK).astype(jnp.int32)
    return slot_tok, block_expert, slots, n_used, n_blocks


def _pad_cols(a, width):
    return jnp.pad(a, ((0, 0), (0, width - a.shape[1])))


def _pad_rows(a, height):
    return jnp.pad(a, ((0, height - a.shape[0]), (0, 0)))


def _rwkv_cols_to_slots(a, lora_w, lora_a):
    W = RWKV_WIDTH
    return jnp.concatenate([a[:, :3 * W], _pad_cols(a[:, 3 * W:3 * W + lora_w], LANE),
                            _pad_cols(a[:, 3 * W + lora_w:3 * W + lora_w + lora_a], LANE),
                            a[:, 3 * W + lora_w + lora_a:]], axis=1)


def _rwkv_slots_to_cols(a, lora_w, lora_a):
    W = RWKV_WIDTH
    return jnp.concatenate([a[:, :3 * W], a[:, 3 * W:3 * W + lora_w],
                            a[:, 3 * W + LANE:3 * W + LANE + lora_a], a[:, 3 * W + 2 * LANE:]], axis=1)


def kernel(x_prompt, x_sample, mem_prompt, cache_fox_k, cache_fox_v, cache_fox_logf, page_table, state_rwkv, state_rwkv_shift, cache_mem_k, cache_mem_v, norm_mix, w_in, fox_q_gain, fox_k_gain, fox_b_f, rwkv_mu, rwkv_w0, rwkv_w2, rwkv_a0, rwkv_a2, rwkv_g2, rwkv_k_k, rwkv_k_a, rwkv_r_k, rwkv_gn_w, rwkv_gn_b, w_out, norm_mem_q, norm_mem_src, w_mem_q, w_mem_kv, mem_q_gain, mem_k_gain, w_mem_o, norm_ffn, w_router_group, w_router_expert, w_exp_gate, w_exp_up, w_exp_down):
    n_b, seq, d_model = x_prompt.shape
    n_dec, dec_seq, _ = x_sample.shape
    assert n_b == 1
    n_p = n_b * seq
    n_s = n_dec * dec_seq
    n_tok = n_p + n_s
    lora_w = rwkv_w2.shape[0]
    lora_a = rwkv_a2.shape[0]
    W = RWKV_WIDTH

    x = jnp.concatenate([x_prompt.reshape(n_p, d_model), x_sample.reshape(n_s, d_model)], axis=0)

    fox_w, rwkv_w = w_in[:, :3 * FOX_WIDTH + FOX_HEADS], w_in[:, 3 * FOX_WIDTH + FOX_HEADS:]
    w1 = jnp.concatenate([fox_w[:, :3 * FOX_WIDTH], _rwkv_cols_to_slots(rwkv_w, lora_w, lora_a),
                          _pad_cols(fox_w[:, 3 * FOX_WIDTH:], LANE)], axis=1)
    w1 = _pad_cols(w1, PROJ_COLS_PADDED).astype(BF16)
    head_gain = jnp.concatenate([jnp.tile(fox_q_gain, FOX_HEADS), jnp.tile(fox_k_gain, FOX_HEADS),
                                 jnp.ones((PROJ_COLS_PADDED - 2 * FOX_WIDTH,), F32)])
    proj = fused_matmul([x], [w1], norm_gain=norm_mix, head_gain=head_gain,
                        n_head_tiles=2 * FOX_WIDTH // PROJ_TN, tn=PROJ_TN, name="proj_in")

    logf_p, c_p = fox_logf_cumsum(proj, 0, n_p, fox_b_f)
    logf_s, _ = fox_logf_cumsum(proj, n_p, n_s, fox_b_f)
    fox_out_p = fox_prompt_attention(proj, c_p, seq)
    fox_out_s = fox_decode_attention(proj, n_p, logf_s, cache_fox_k, cache_fox_v, cache_fox_logf,
                                     page_table, dec_seq)
    fox_out = jnp.concatenate([fox_out_p, fox_out_s], axis=0)

    p_rwkv = proj[:, COL_RWKV:COL_RWKV + RWKV_COLS_PADDED]
    shift_s = _rwkv_cols_to_slots(state_rwkv_shift, lora_w, lora_a)
    p_s = p_rwkv[n_p:].reshape(n_dec, dec_seq, RWKV_COLS_PADDED)
    short_first = jnp.zeros((n_dec, dec_seq, RWKV_COLS_PADDED), F32).at[:, 0].set(shift_s)
    mu = _rwkv_cols_to_slots(rwkv_mu.reshape(1, -1), lora_w, lora_a)
    r, dec, k2, v, a_vec, b_vec, g = rwkv_prep(
        proj, n_p, short_first.reshape(n_s, RWKV_COLS_PADDED), dec_seq, mu, rwkv_w0,
        _pad_rows(rwkv_w2, LANE).astype(BF16), rwkv_a0,
        _pad_rows(rwkv_a2, LANE).astype(BF16), rwkv_g2.astype(BF16), rwkv_k_k, rwkv_k_a)
    seqs = (r, dec, k2, v, a_vec, b_vec)
    s0_p = jnp.zeros((n_b, RWKV_HEAD_DIM, W), F32)
    s0_s = state_rwkv.astype(F32).transpose(0, 2, 1, 3).reshape(n_dec, RWKV_HEAD_DIM, W)
    o_p, st_p = rwkv_scan(seqs, 0, n_b, seq, s0_p)
    o_s, st_s = rwkv_scan(seqs, n_p, n_dec, dec_seq, s0_s, n_par=8 if n_dec % 8 == 0 else 1,
                          unroll=1)
    o = jnp.concatenate([o_p, o_s], axis=0)
    rwkv_out = rwkv_post(o, r, k2, v, g, rwkv_r_k, rwkv_gn_w, rwkv_gn_b)
    unstate = lambda s: s.reshape(-1, RWKV_HEAD_DIM, RWKV_HEADS, RWKV_HEAD_DIM).transpose(0, 2, 1, 3)

    w_out_b = w_out.astype(BF16)
    h1 = fused_matmul([fox_out, rwkv_out], [w_out_b[:FOX_WIDTH], w_out_b[FOX_WIDTH:]], residual=x,
                      name="proj_out")

    mem_kv = fused_matmul([mem_prompt.reshape(-1, d_model)], [w_mem_kv.astype(BF16)],
                          norm_gain=norm_mem_src,
                          head_gain=jnp.concatenate([jnp.tile(mem_k_gain, MEM_HEADS),
                                                     jnp.ones((MEM_WIDTH,), F32)]),
                          n_head_tiles=1, tn=MEM_WIDTH, name="mem_kv")
    n_mem = mem_prompt.shape[1]
    mem_k_p = mem_kv[:, :MEM_WIDTH].reshape(n_b, n_mem, MEM_WIDTH)
    mem_v_p = mem_kv[:, MEM_WIDTH:].reshape(n_b, n_mem, MEM_WIDTH)
    q_mem = fused_matmul([h1], [w_mem_q.astype(BF16)], norm_gain=norm_mem_q,
                         head_gain=jnp.tile(mem_q_gain, MEM_HEADS), n_head_tiles=1, tn=MEM_WIDTH,
                         name="mem_q")
    att_p = memory_attention(q_mem, 0, n_b, seq, mem_k_p, mem_v_p)
    att_s = memory_attention(q_mem, n_p, n_dec, dec_seq,
                             cache_mem_k.reshape(n_dec, -1, MEM_WIDTH),
                             cache_mem_v.reshape(n_dec, -1, MEM_WIDTH))
    h2 = fused_matmul([jnp.concatenate([att_p, att_s], axis=0)], [w_mem_o.astype(BF16)],
                      residual=h1, name="mem_out")

    w_router = _pad_cols(jnp.concatenate([w_router_group, w_router_expert], axis=1), LANE)
    logits, xn = fused_matmul([h2], [w_router], norm_gain=norm_ffn, emit_lhs=True, exact=True,
                              tn=LANE, name="moe_logits")
    idx, wts = moe_router(logits)
    slot_tok, block_expert, slots, n_used, n_blocks = moe_dispatch_tables(idx[:, :TOP_K], n_tok)
    xn_pad = jnp.concatenate([xn, jnp.zeros((8, d_model), F32)], axis=0)
    yb = moe_experts(xn_pad, slot_tok, w_exp_gate.astype(BF16), w_exp_up.astype(BF16),
                     w_exp_down.astype(BF16), block_expert, n_used, n_blocks)
    y = moe_combine(yb, h2, wts, slots)

    heads = lambda t, b, s: t.reshape(b, s, FOX_HEADS, FOX_HEAD_DIM)
    p_rwkv_last_p = _rwkv_slots_to_cols(p_rwkv[n_p - 1:n_p], lora_w, lora_a)
    p_rwkv_last_s = _rwkv_slots_to_cols(p_s[:, -1], lora_w, lora_a)
    return (y[:n_p].reshape(n_b, seq, d_model), y[n_p:].reshape(n_dec, dec_seq, d_model),
            heads(proj[:n_p, COL_K:COL_K + FOX_WIDTH], n_b, seq),
            heads(proj[:n_p, COL_V:COL_V + FOX_WIDTH], n_b, seq),
            logf_p.reshape(n_b, seq, FOX_HEADS),
            unstate(st_p), p_rwkv_last_p,
            mem_k_p.reshape(n_b, n_mem, MEM_HEADS, MEM_HEAD_DIM),
            mem_v_p.reshape(n_b, n_mem, MEM_HEADS, MEM_HEAD_DIM),
            heads(proj[n_p:, COL_K:COL_K + FOX_WIDTH], n_dec, dec_seq),
            heads(proj[n_p:, COL_V:COL_V + FOX_WIDTH], n_dec, dec_seq),
            logf_s.reshape(n_dec, dec_seq, FOX_HEADS),
            unstate(st_s), p_rwkv_last_s)
```

```python
import functools

import numpy as np
import jax
import jax.numpy as jnp
from jax import lax
from jax.experimental import pallas as pl
from jax.experimental.pallas import tpu as pltpu

F32 = jnp.float32
BF16 = jnp.bfloat16
HI = lax.Precision.HIGHEST

LANE = 128
FOX_HEADS = 16
FOX_HEAD_DIM = 128
FOX_WIDTH = FOX_HEADS * FOX_HEAD_DIM
RWKV_HEADS = 32
RWKV_HEAD_DIM = 64
RWKV_WIDTH = RWKV_HEADS * RWKV_HEAD_DIM
MEM_HEADS = 4
MEM_HEAD_DIM = 128
MEM_WIDTH = MEM_HEADS * MEM_HEAD_DIM
N_GROUPS = 8
EXPERTS_PER_GROUP = 8
N_EXPERTS = N_GROUPS * EXPERTS_PER_GROUP
TOP_K = 2
EXPERT_BLOCK = 128
NORM_EPS = 1e-6
RWKV_GN_EPS = 64e-5
FOX_SCALE = FOX_HEAD_DIM ** -0.5
LOG2E = 1.4426950408889634
MEM_SCALE = MEM_HEAD_DIM ** -0.5
VMEM_LIMIT = 56 * 1024 * 1024

COL_Q = 0
COL_K = FOX_WIDTH
COL_V = 2 * FOX_WIDTH
COL_RWKV = 3 * FOX_WIDTH
LORA_SLOTS = 2 * LANE + 256
RWKV_COLS_PADDED = 3 * RWKV_WIDTH + LORA_SLOTS
COL_F = COL_RWKV + RWKV_COLS_PADDED
PROJ_COLS = COL_F + LANE
PROJ_TN = 512
PROJ_COLS_PADDED = -(-PROJ_COLS // PROJ_TN) * PROJ_TN


def _cparams(sem):
    return pltpu.CompilerParams(dimension_semantics=sem, vmem_limit_bytes=VMEM_LIMIT)


def _dot(a, b, precision=None):
    return jnp.dot(a, b, preferred_element_type=F32, precision=precision)


def _dot_nt(a, b, precision=None):
    return lax.dot_general(a, b, (((1,), (1,)), ((), ())), preferred_element_type=F32,
                           precision=precision)


def _iota(shape, dim):
    return lax.broadcasted_iota(jnp.int32, shape, dim)


def _mm_body(*refs, n_lhs, has_norm, has_res, n_head_tiles, emit_lhs, exact, tn):
    it = iter(refs)
    lhs = [next(it) for _ in range(n_lhs)]
    gain = next(it) if has_norm else None
    ws = [next(it) for _ in range(n_lhs)]
    res = next(it) if has_res else None
    hgain = next(it) if n_head_tiles else None
    out = next(it)
    lhs_out = next(it) if emit_lhs else None
    xs = [next(it) for _ in range(n_lhs)]
    j = pl.program_id(1)

    @pl.when(j == 0)
    def _():
        for l_ref, x_ref in zip(lhs, xs):
            v = l_ref[...]
            if has_norm:
                v = v * lax.rsqrt(jnp.mean(v * v, axis=-1, keepdims=True) + NORM_EPS) * gain[...]
            x_ref[...] = v.astype(x_ref.dtype)
            if emit_lhs:
                lhs_out[...] = v

    acc = None
    for x_ref, w_ref in zip(xs, ws):
        d = _dot(x_ref[...], w_ref[...], HI if exact else None)
        acc = d if acc is None else acc + d
    if has_res:
        acc = acc + res[...]
    if n_head_tiles:
        @pl.when(j < n_head_tiles)
        def _():
            g = hgain[...]
            for s in range(tn // LANE):
                seg = acc[:, s * LANE:(s + 1) * LANE]
                ms = jnp.mean(seg * seg, axis=-1, keepdims=True)
                out[:, s * LANE:(s + 1) * LANE] = (seg * lax.rsqrt(ms + NORM_EPS)
                                                   * g[:, s * LANE:(s + 1) * LANE])

        @pl.when(j >= n_head_tiles)
        def _():
            out[...] = acc
    else:
        out[...] = acc


def fused_matmul(lhs, ws, *, norm_gain=None, residual=None, head_gain=None, n_head_tiles=0,
                 emit_lhs=False, exact=False, tm=512, tn=512, name="fused_matmul"):
    n_lhs = len(lhs)
    m = lhs[0].shape[0]
    n = ws[0].shape[1]
    tm = min(tm, m)
    tn = min(tn, n)
    assert m % tm == 0 and n % tn == 0
    has_norm = norm_gain is not None
    has_res = residual is not None
    assert not (has_norm and n_lhs != 1)
    in_specs, args = [], []
    for l in lhs:
        in_specs.append(pl.BlockSpec((tm, l.shape[1]), lambda i, j: (i, 0)))
        args.append(l)
    if has_norm:
        in_specs.append(pl.BlockSpec((1, lhs[0].shape[1]), lambda i, j: (0, 0)))
        args.append(norm_gain.reshape(1, -1).astype(F32))
    for w in ws:
        in_specs.append(pl.BlockSpec((w.shape[0], tn), lambda i, j: (0, j)))
        args.append(w)
    if has_res:
        in_specs.append(pl.BlockSpec((tm, tn), lambda i, j: (i, j)))
        args.append(residual)
    if n_head_tiles:
        in_specs.append(pl.BlockSpec((1, tn), lambda i, j: (0, j)))
        args.append(head_gain.reshape(1, -1).astype(F32))
    out_shape = [jax.ShapeDtypeStruct((m, n), F32)]
    out_specs = [pl.BlockSpec((tm, tn), lambda i, j: (i, j))]
    if emit_lhs:
        out_shape.append(jax.ShapeDtypeStruct((m, lhs[0].shape[1]), F32))
        out_specs.append(pl.BlockSpec((tm, lhs[0].shape[1]), lambda i, j: (i, 0)))
    scratch = [pltpu.VMEM((tm, l.shape[1]), F32 if exact else BF16) for l in lhs]
    body = functools.partial(_mm_body, n_lhs=n_lhs, has_norm=has_norm, has_res=has_res,
                             n_head_tiles=n_head_tiles, emit_lhs=emit_lhs, exact=exact, tn=tn)
    res = pl.pallas_call(
        body, grid=(m // tm, n // tn), in_specs=in_specs, out_specs=out_specs,
        out_shape=out_shape, scratch_shapes=scratch,
        compiler_params=_cparams(("parallel", "arbitrary")), name=name)(*args)
    return res if emit_lhs else res[0]


def _log_sigmoid(x):
    return jnp.minimum(x, 0.0) - jnp.log1p(jnp.exp(-jnp.abs(x)))


def _logf_body(f_ref, b_ref, logf_ref, c_ref, carry, *, tc):
    i = pl.program_id(0)

    @pl.when(i == 0)
    def _():
        carry[...] = jnp.zeros_like(carry)

    lf = _log_sigmoid(f_ref[:, :FOX_HEADS] + b_ref[...])
    logf_ref[...] = lf
    tri = (_iota((tc, tc), 0) >= _iota((tc, tc), 1)).astype(F32)
    c = _dot(tri, lf, HI) + carry[...]
    c_ref[...] = c
    carry[...] = c[tc - 1:tc, :]


def fox_logf_cumsum(proj, row0, rows, b_f, *, tc=512):
    tc = min(tc, rows)
    assert rows % tc == 0 and row0 % tc == 0
    rb0 = row0 // tc
    return pl.pallas_call(
        functools.partial(_logf_body, tc=tc), grid=(rows // tc,),
        in_specs=[pl.BlockSpec((tc, LANE), lambda i: (rb0 + i, COL_F // LANE)),
                  pl.BlockSpec((1, FOX_HEADS), lambda i: (0, 0))],
        out_specs=[pl.BlockSpec((tc, FOX_HEADS), lambda i: (i, 0)),
                   pl.BlockSpec((tc, FOX_HEADS), lambda i: (i, 0))],
        out_shape=[jax.ShapeDtypeStruct((rows, FOX_HEADS), F32)] * 2,
        scratch_shapes=[pltpu.VMEM((1, FOX_HEADS), F32)],
        compiler_params=_cparams(("arbitrary",)), name="fox_logf_cumsum")(
            proj, b_f.reshape(1, FOX_HEADS).astype(F32))


def _flash_body(qi_tbl, ki_tbl, q_ref, k_ref, v_ref, c_ref, ct_ref, o_ref, m_s, l_s, acc_s, cq_s,
                *, tq, tk):
    h = pl.program_id(0)
    s_idx = pl.program_id(1)
    qi = qi_tbl[s_idx]
    ki = ki_tbl[s_idx]

    @pl.when(ki == 0)
    def _():
        m_s[...] = jnp.full_like(m_s, -jnp.inf)
        l_s[...] = jnp.zeros_like(l_s)
        acc_s[...] = jnp.zeros_like(acc_s)
        lane = _iota((tq, FOX_HEADS), 1)
        cq_s[...] = jnp.sum(jnp.where(lane == h, c_ref[...], 0.0), axis=1, keepdims=True)

    def update(masked):
        q = (q_ref[...] * (FOX_SCALE * LOG2E)).astype(BF16)
        s = _dot_nt(q, k_ref[...].astype(BF16)) - ct_ref[...] * LOG2E
        if masked:
            q_pos = qi * tq + _iota((tq, tk), 0)
            k_pos = ki * tk + _iota((tq, tk), 1)
            s = jnp.where(k_pos <= q_pos, s, -jnp.inf)
        cq = cq_s[...] * LOG2E
        m_prev = m_s[...]
        m_new = jnp.maximum(m_prev, jnp.max(s, axis=1, keepdims=True) + cq)
        alpha = jnp.exp2(m_prev - m_new)
        p = jnp.exp2(s - (m_new - cq))
        l_s[...] = alpha * l_s[...] + jnp.sum(p, axis=1, keepdims=True)
        acc_s[...] = alpha * acc_s[...] + _dot(p.astype(BF16), v_ref[...].astype(BF16))
        m_s[...] = m_new

    crosses = (ki + 1) * tk - 1 > qi * tq

    @pl.when(crosses)
    def _():
        update(True)

    @pl.when(jnp.logical_not(crosses))
    def _():
        update(False)

    @pl.when(ki == (qi + 1) * (tq // tk) - 1)
    def _():
        o_ref[...] = acc_s[...] / l_s[...]


def fox_prompt_attention(proj, c, seq, *, tq=1024, tk=1024):
    tq = min(tq, seq)
    tk = min(tk, tq)
    assert seq % tq == 0 and tq % tk == 0
    nq = seq // tq
    qi_np, ki_np = [], []
    for qi in range(nq):
        for ki in range((qi + 1) * (tq // tk)):
            qi_np.append(qi)
            ki_np.append(ki)
    qi_tbl = jnp.asarray(np.array(qi_np, np.int32))
    ki_tbl = jnp.asarray(np.array(ki_np, np.int32))
    ct = c.T.reshape(FOX_HEADS, 1, seq)
    grid_spec = pltpu.PrefetchScalarGridSpec(
        num_scalar_prefetch=2, grid=(FOX_HEADS, len(qi_np)),
        in_specs=[
            pl.BlockSpec((tq, FOX_HEAD_DIM), lambda h, s, qt, kt: (qt[s], COL_Q // LANE + h)),
            pl.BlockSpec((tk, FOX_HEAD_DIM), lambda h, s, qt, kt: (kt[s], COL_K // LANE + h)),
            pl.BlockSpec((tk, FOX_HEAD_DIM), lambda h, s, qt, kt: (kt[s], COL_V // LANE + h)),
            pl.BlockSpec((tq, FOX_HEADS), lambda h, s, qt, kt: (qt[s], 0)),
            pl.BlockSpec((None, 1, tk), lambda h, s, qt, kt: (h, 0, kt[s])),
        ],
        out_specs=pl.BlockSpec((tq, FOX_HEAD_DIM), lambda h, s, qt, kt: (qt[s], h)),
        scratch_shapes=[pltpu.VMEM((tq, 1), F32), pltpu.VMEM((tq, 1), F32),
                        pltpu.VMEM((tq, FOX_HEAD_DIM), F32), pltpu.VMEM((tq, 1), F32)])
    return pl.pallas_call(
        functools.partial(_flash_body, tq=tq, tk=tk), grid_spec=grid_spec,
        out_shape=jax.ShapeDtypeStruct((seq, FOX_WIDTH), F32),
        compiler_params=_cparams(("parallel", "arbitrary")), name="fox_prompt_attention")(
            qi_tbl, ki_tbl, proj, proj, proj, c, ct)


def _page_bias_body(lf_ref, sfx_ref, tot_ref, *, page):
    H = FOX_HEADS
    lf = lf_ref[...]
    n_lane = lf.shape[1]
    lane = _iota(lf.shape, 1)

    def later(x, sh):
        return jnp.where(lane < n_lane - sh, pltpu.roll(x, n_lane - sh, 1), 0.0)

    total = lf
    sfx = later(lf, H)
    for kk in range(page.bit_length() - 1):
        total = total + pltpu.roll(total, H << kk, 1)
        sfx = sfx + later(sfx, H << kk)
    sfx_ref[...] = sfx
    tot_ref[...] = total


def fox_page_bias(cache_logf):
    n_pool, page, _ = cache_logf.shape
    assert page & (page - 1) == 0
    lf = cache_logf.reshape(n_pool, page * FOX_HEADS)
    tp = max(t for t in range(8, min(n_pool, 256) + 1, 8) if n_pool % t == 0)
    blk = pl.BlockSpec((tp, page * FOX_HEADS), lambda i: (i, 0))
    return pl.pallas_call(
        functools.partial(_page_bias_body, page=page), grid=(n_pool // tp,), in_specs=[blk],
        out_specs=[blk, blk], out_shape=[jax.ShapeDtypeStruct(lf.shape, F32)] * 2,
        compiler_params=_cparams(("parallel",)), name="fox_page_bias")(lf)


def _decode_body(pt_ref, q_ref, kn_ref, vn_ref, lfn_ref, mask_ref, *refs,
                 n_steps, pages_per_step, tnew):
    page_refs = refs[:4 * pages_per_step]
    o_ref, q_s, m_s, l_s, acc_s, carry_s, ccol_s = refs[4 * pages_per_step:]
    p = pl.program_id(1)
    H = FOX_HEADS
    D = FOX_HEAD_DIM
    rows = H * tnew

    def head_rows(x):
        return jnp.concatenate([x[:, h * D:(h + 1) * D] for h in range(H)], axis=0)

    @pl.when(p == 0)
    def _():
        q2 = head_rows(q_ref[...] * FOX_SCALE).astype(BF16)
        q_s[...] = q2
        kn = head_rows(kn_ref[...]).astype(BF16)
        vn = head_rows(vn_ref[...]).astype(BF16)
        spread = (_iota((H, rows), 0) == _iota((H, rows), 1) // tnew).astype(F32)
        lf_exp = _dot(lfn_ref[...], spread, HI)
        upto = _iota((tnew, rows), 0) <= _iota((tnew, rows), 1) % tnew
        cn_row = jnp.sum(jnp.where(upto, lf_exp, 0.0), axis=0, keepdims=True)
        ri = _iota((rows, rows), 0)
        li = _iota((rows, rows), 1)
        ccol = jnp.sum(jnp.where(ri == li, cn_row, 0.0), axis=1, keepdims=True)
        ccol_s[...] = ccol
        s = _dot_nt(q2, kn) + ccol - cn_row
        valid = jnp.logical_and(ri // tnew == li // tnew, li % tnew <= ri % tnew)
        s = jnp.where(valid, s, -jnp.inf)
        m = jnp.max(s, axis=1, keepdims=True)
        pr = jnp.exp(s - m)
        m_s[...] = m
        l_s[...] = jnp.sum(pr, axis=1, keepdims=True)
        acc_s[...] = _dot(pr.astype(BF16), vn)
        carry_s[...] = jnp.zeros_like(carry_s)

    ccol = ccol_s[...]
    for j in range(pages_per_step):
        kc_ref, vc_ref, sfx_ref, tot_ref = page_refs[4 * j:4 * j + 4]
        bias = sfx_ref[...] + carry_s[...]
        s = _dot_nt(q_s[...], kc_ref[...].astype(BF16)) + (mask_ref[...] + bias)
        m_prev = m_s[...]
        m_new = jnp.maximum(m_prev, jnp.max(s, axis=1, keepdims=True) + ccol)
        alpha = jnp.exp(m_prev - m_new)
        pr = jnp.exp(s - (m_new - ccol))
        l_s[...] = alpha * l_s[...] + jnp.sum(pr, axis=1, keepdims=True)
        acc_s[...] = alpha * acc_s[...] + _dot(pr.astype(BF16), vc_ref[...].astype(BF16))
        m_s[...] = m_new
        carry_s[...] = carry_s[...] + tot_ref[...]

    @pl.when(p == n_steps - 1)
    def _():
        o = acc_s[...] / l_s[...]
        o_ref[...] = jnp.concatenate([o[h * tnew:(h + 1) * tnew, :] for h in range(H)], axis=1)


def fox_decode_attention(proj, row0, logf_new, cache_k, cache_v, cache_logf, page_table, tnew):
    n_seq, n_pages = page_table.shape
    n_pool, page = cache_k.shape[0], cache_k.shape[1]
    assert row0 % tnew == 0 and tnew == 8 and page & (page - 1) == 0
    rb0 = row0 // tnew
    kc = cache_k.reshape(n_pool, page * FOX_HEADS, FOX_HEAD_DIM)
    vc = cache_v.reshape(n_pool, page * FOX_HEADS, FOX_HEAD_DIM)
    n_lane = page * FOX_HEADS
    sfx, tot = fox_page_bias(cache_logf)
    sfx = sfx.reshape(n_pool, 1, n_lane)
    tot = tot.reshape(n_pool, 1, n_lane)
    rows = FOX_HEADS * tnew
    keep = (np.arange(n_lane)[None, :] % FOX_HEADS) == (np.arange(rows)[:, None] // tnew)
    mask = jnp.asarray(np.where(keep, 0.0, -np.inf).astype(np.float32))
    pps = max(c for c in (4, 2, 1) if n_pages % c == 0)
    n_steps = n_pages // pps

    def paged(j):
        return lambda b, p, pt: (pt[b, n_pages - 1 - (p * pps + j)], 0, 0)

    page_specs, page_args = [], []
    for j in range(pps):
        page_specs += [pl.BlockSpec((None, n_lane, FOX_HEAD_DIM), paged(j)),
                       pl.BlockSpec((None, n_lane, FOX_HEAD_DIM), paged(j)),
                       pl.BlockSpec((None, 1, n_lane), paged(j)),
                       pl.BlockSpec((None, 1, n_lane), paged(j))]
        page_args += [kc, vc, sfx, tot]
    grid_spec = pltpu.PrefetchScalarGridSpec(
        num_scalar_prefetch=1, grid=(n_seq, n_steps),
        in_specs=[
            pl.BlockSpec((tnew, FOX_WIDTH), lambda b, p, pt: (rb0 + b, COL_Q // FOX_WIDTH)),
            pl.BlockSpec((tnew, FOX_WIDTH), lambda b, p, pt: (rb0 + b, COL_K // FOX_WIDTH)),
            pl.BlockSpec((tnew, FOX_WIDTH), lambda b, p, pt: (rb0 + b, COL_V // FOX_WIDTH)),
            pl.BlockSpec((tnew, FOX_HEADS), lambda b, p, pt: (b, 0)),
            pl.BlockSpec((rows, n_lane), lambda b, p, pt: (0, 0)),
        ] + page_specs,
        out_specs=pl.BlockSpec((tnew, FOX_WIDTH), lambda b, p, pt: (b, 0)),
        scratch_shapes=[pltpu.VMEM((rows, FOX_HEAD_DIM), BF16), pltpu.VMEM((rows, 1), F32),
                        pltpu.VMEM((rows, 1), F32), pltpu.VMEM((rows, FOX_HEAD_DIM), F32),
                        pltpu.VMEM((1, n_lane), F32), pltpu.VMEM((rows, 1), F32)])
    return pl.pallas_call(
        functools.partial(_decode_body, n_steps=n_steps, pages_per_step=pps, tnew=tnew),
        grid_spec=grid_spec, out_shape=jax.ShapeDtypeStruct((n_seq * tnew, FOX_WIDTH), F32),
        compiler_params=_cparams(("parallel", "arbitrary")), name="fox_decode_attention")(
            page_table, proj, proj, proj, logf_new, mask, *page_args)


def _head_sum_matrix(width):
    return (_iota((width, width), 0) // RWKV_HEAD_DIM
            == _iota((width, width), 1) // RWKV_HEAD_DIM)


def _head_sum_exact(x):
    pm = _head_sum_matrix(LANE).astype(F32)
    return jnp.concatenate([_dot(x[:, j * LANE:(j + 1) * LANE], pm, HI)
                            for j in range(x.shape[1] // LANE)], axis=1)


def _softplus(y):
    return jnp.maximum(y, 0.0) + jnp.log1p(jnp.exp(-jnp.abs(y)))


def _rwkv_prep_body(pr_ref, pk_ref, pv_ref, pl_ref, er_ref, ek_ref, ev_ref, el_ref,
                    fr_ref, fk_ref, fv_ref, fl_ref, mr_ref, mk_ref, mv_ref, ml_ref,
                    w0_ref, w2_ref, a0_ref, a2_ref, g2_ref, kk_ref, ka_ref,
                    r_o, w_o, k_o, v_o, a_o, b_o, g_o, *, tm, n_long_tiles, short_len):
    i = pl.program_id(0)
    is_short = i >= n_long_tiles

    def shift(p_ref, e_ref, f_ref, m_ref):
        p = p_ref[...]
        rolled = pltpu.roll(p, 1, 0)
        row = _iota(p.shape, 0)
        before_tile = jnp.where(i == 0, 0.0, e_ref[7:8, :])
        long_prev = jnp.where(row == 0, before_tile, rolled)
        short_prev = jnp.where(row % short_len == 0, f_ref[...], rolled)
        prev = jnp.where(is_short, short_prev, long_prev)
        return p + m_ref[...] * (prev - p)

    r = shift(pr_ref, er_ref, fr_ref, mr_ref)
    k = shift(pk_ref, ek_ref, fk_ref, mk_ref)
    v = shift(pv_ref, ev_ref, fv_ref, mv_ref)
    zl = shift(pl_ref, el_ref, fl_ref, ml_ref)
    xw = zl[:, 0:LANE]
    xa = zl[:, LANE:2 * LANE]
    xg = zl[:, 2 * LANE:]
    lw = w0_ref[...] + _dot(jnp.tanh(xw).astype(BF16), w2_ref[...])
    w_log = -_softplus(-lw) - 0.5
    a = jax.nn.sigmoid(a0_ref[...] + _dot(xa.astype(BF16), a2_ref[...]))
    kk = k * kk_ref[...]
    kk = kk * lax.rsqrt(jnp.maximum(_head_sum_exact(kk * kk), 1e-24))
    r_o[...] = r
    w_o[...] = jnp.exp(-jnp.exp(w_log))
    k_o[...] = k * (1.0 + (a - 1.0) * ka_ref[...])
    v_o[...] = v
    a_o[...] = -kk
    b_o[...] = kk * a
    g_o[...] = _dot(jax.nn.sigmoid(xg).astype(BF16), g2_ref[...])


def rwkv_prep(proj, n_long, short_first, short_len, mu, w0, w2, a0, a2, g2, k_k, k_a, *, tm=128):
    m = proj.shape[0]
    tm = min(tm, m)
    W = RWKV_WIDTH
    assert m % tm == 0 and n_long % tm == 0 and tm % short_len == 0 and tm % 8 == 0
    assert COL_RWKV % W == 0 and (COL_RWKV + 3 * W) % LORA_SLOTS == 0
    n_long_tiles = n_long // tm
    row = lambda a: a.reshape(1, -1).astype(F32)
    vec = pl.BlockSpec((1, W), lambda i: (0, 0))

    def pieces(rows, col0, row_block):
        specs = [pl.BlockSpec((rows, W), functools.partial(lambda i, c: (row_block(i), c),
                                                           c=col0 // W + n)) for n in range(3)]
        specs.append(pl.BlockSpec((rows, LORA_SLOTS),
                                  lambda i: (row_block(i), (col0 + 3 * W) // LORA_SLOTS)))
        return specs

    body = functools.partial(_rwkv_prep_body, tm=tm, n_long_tiles=n_long_tiles,
                             short_len=short_len)
    return pl.pallas_call(
        body, grid=(m // tm,),
        in_specs=(pieces(tm, COL_RWKV, lambda i: i)
                  + pieces(8, COL_RWKV, lambda i: jnp.maximum(i * (tm // 8) - 1, 0))
                  + pieces(tm, 0, lambda i: jnp.maximum(i - n_long_tiles, 0))
                  + pieces(1, 0, lambda i: 0)
                  + [vec, pl.BlockSpec((LANE, W), lambda i: (0, 0)), vec,
                     pl.BlockSpec((LANE, W), lambda i: (0, 0)),
                     pl.BlockSpec((256, W), lambda i: (0, 0)), vec, vec]),
        out_specs=[pl.BlockSpec((tm, W), lambda i: (i, 0))] * 7,
        out_shape=[jax.ShapeDtypeStruct((m, W), F32)] * 7,
        compiler_params=_cparams(("parallel",)), name="rwkv_prep")(
            *([proj] * 8 + [short_first] * 4 + [mu] * 4),
            row(w0), w2, row(a0), a2, g2, row(k_k), row(k_a))


def _scan_body(r_ref, w_ref, k_ref, v_ref, a_ref, b_ref, s0_ref, o_ref, st_ref, state, *, tc,
               n_chunks, n_par, unroll):
    c = pl.program_id(1)
    N = RWKV_HEAD_DIM
    W = RWKV_WIDTH
    SEG = 256
    n_seg = W // SEG

    @pl.when(c == 0)
    def _():
        state[...] = s0_ref[...]

    diag = (_iota((N, W), 1) % N == _iota((N, W), 0)).astype(F32)
    pm = _head_sum_matrix(SEG).astype(BF16)

    def head_sums(parts):
        n = len(parts)
        tiles = [p[:, j * SEG:(j + 1) * SEG] for j in range(n_seg) for p in parts]
        y = _dot(jnp.concatenate(tiles, axis=0), pm)
        return [jnp.concatenate([y[(j * n + i) * N:(j * n + i + 1) * N, :]
                                 for j in range(n_seg)], axis=1) for i in range(n)]

    def row(ref, g, t):
        return ref[pl.ds(g * tc + t, 1), :]

    def out_row(o_bc):
        return jnp.sum(o_bc * diag, axis=0, keepdims=True)

    def advance(t, with_prev):
        parts, n_each = [], 3 if with_prev else 2
        for g in range(n_par):
            s = state[g]
            parts += [(s * row(a_ref, g, t)).astype(BF16), (row(v_ref, g, t) * diag).astype(BF16)]
            if with_prev:
                parts.append((s * row(r_ref, g, t - 1)).astype(BF16))
        sums = head_sums(parts)
        for g in range(n_par):
            sa, vc = sums[g * n_each:g * n_each + 2]
            state[g] = (state[g] * row(w_ref, g, t) + sa * row(b_ref, g, t)
                        + vc * row(k_ref, g, t))
            if with_prev:
                o_ref[pl.ds(g * tc + t - 1, 1), :] = out_row(sums[g * n_each + 2])

    advance(0, False)

    def step(t, carry):
        advance(t, True)
        return carry

    lax.fori_loop(1, tc, step, 0, unroll=unroll)
    last = head_sums([(state[g] * row(r_ref, g, tc - 1)).astype(BF16) for g in range(n_par)])
    for g in range(n_par):
        o_ref[pl.ds(g * tc + tc - 1, 1), :] = out_row(last[g])

    @pl.when(c == n_chunks - 1)
    def _():
        st_ref[...] = state[...]


def rwkv_scan(seqs, row0, n_seq, seq_len, s0, *, tc=256, n_par=1, unroll=2):
    tc = min(tc, seq_len)
    n_chunks = seq_len // tc
    assert seq_len % tc == 0 and row0 % (n_par * tc) == 0 and n_seq % n_par == 0
    assert n_par == 1 or n_chunks == 1
    rb0 = row0 // (n_par * tc)
    W = RWKV_WIDTH
    tok = pl.BlockSpec((n_par * tc, W), lambda s, c: (rb0 + s * n_chunks + c, 0))
    st = pl.BlockSpec((n_par, RWKV_HEAD_DIM, W), lambda s, c: (s, 0, 0))
    return pl.pallas_call(
        functools.partial(_scan_body, tc=tc, n_chunks=n_chunks, n_par=n_par,
                          unroll=min(unroll, tc - 1)),
        grid=(n_seq // n_par, n_chunks),
        in_specs=[tok] * 6 + [st],
        out_specs=[pl.BlockSpec((n_par * tc, W), lambda s, c: (s * n_chunks + c, 0)), st],
        out_shape=[jax.ShapeDtypeStruct((n_seq * seq_len, W), F32),
                   jax.ShapeDtypeStruct((n_seq, RWKV_HEAD_DIM, W), F32)],
        scratch_shapes=[pltpu.VMEM((n_par, RWKV_HEAD_DIM, W), F32)],
        compiler_params=_cparams(("parallel", "arbitrary")), name="rwkv_scan")(*seqs, s0)


def _rwkv_post_body(o_ref, r_ref, k_ref, v_ref, g_ref, rk_ref, gw_ref, gb_ref, out_ref):
    o = o_ref[...]
    inv_n = 1.0 / RWKV_HEAD_DIM
    mean = _head_sum_exact(o) * inv_n
    d = o - mean
    var = _head_sum_exact(d * d) * inv_n
    y = d * lax.rsqrt(var + RWKV_GN_EPS) * gw_ref[...] + gb_ref[...]
    y = y + _head_sum_exact(r_ref[...] * k_ref[...] * rk_ref[...]) * v_ref[...]
    out_ref[...] = y * g_ref[...]


def rwkv_post(o, r, k, v, g, r_k, gn_w, gn_b, *, tm=256):
    m = o.shape[0]
    tm = min(tm, m)
    assert m % tm == 0
    W = RWKV_WIDTH
    row = lambda a: a.reshape(1, -1).astype(F32)
    tok = pl.BlockSpec((tm, W), lambda i: (i, 0))
    vec = pl.BlockSpec((1, W), lambda i: (0, 0))
    return pl.pallas_call(
        _rwkv_post_body, grid=(m // tm,), in_specs=[tok] * 5 + [vec] * 3, out_specs=tok,
        out_shape=jax.ShapeDtypeStruct((m, W), F32),
        compiler_params=_cparams(("parallel",)), name="rwkv_post")(
            o, r, k, v, g, row(r_k), row(gn_w), row(gn_b))


def _memattn_body(q_ref, k_ref, v_ref, o_ref):
    q = q_ref[...]
    k = k_ref[...]
    v = v_ref[...]
    outs = []
    for h in range(MEM_HEADS):
        sl = slice(h * MEM_HEAD_DIM, (h + 1) * MEM_HEAD_DIM)
        s = _dot_nt(q[:, sl].astype(BF16), k[:, sl].astype(BF16)) * MEM_SCALE
        e = jnp.exp(s - jnp.max(s, axis=1, keepdims=True))
        pr = e / jnp.sum(e, axis=1, keepdims=True)
        outs.append(_dot(pr.astype(BF16), v[:, sl].astype(BF16)))
    o_ref[...] = jnp.concatenate(outs, axis=1)


def memory_attention(q, row0, n_seq, seq_len, mem_k, mem_v, *, tq=512):
    tq = min(tq, seq_len)
    assert seq_len % tq == 0 and row0 % tq == 0
    nt = seq_len // tq
    rb0 = row0 // tq
    m_tok = mem_k.shape[1]
    kv = pl.BlockSpec((None, m_tok, MEM_WIDTH), lambda b, t: (b, 0, 0))
    return pl.pallas_call(
        _memattn_body, grid=(n_seq, nt),
        in_specs=[pl.BlockSpec((tq, MEM_WIDTH), lambda b, t: (rb0 + b * nt + t, 0)), kv, kv],
        out_specs=pl.BlockSpec((tq, MEM_WIDTH), lambda b, t: (b * nt + t, 0)),
        out_shape=jax.ShapeDtypeStruct((n_seq * seq_len, MEM_WIDTH), F32),
        compiler_params=_cparams(("parallel", "parallel")), name="memory_attention")(
            q, mem_k, mem_v)


def _router_body(lg_ref, idx_ref, wt_ref):
    lg = lg_ref[...]
    shape = lg.shape
    lane = _iota(shape, 1).astype(F32)
    neg = -jnp.inf
    big = float(LANE)
    is_g = lane < N_GROUPS
    gl = jnp.where(is_g, lg, neg)
    gmax = jnp.max(gl, axis=1, keepdims=True)
    gsel = jnp.min(jnp.where(gl == gmax, lane, big), axis=1, keepdims=True)
    gsum = jnp.sum(jnp.where(is_g, jnp.exp(lg - gmax), 0.0), axis=1, keepdims=True)
    g_w = 1.0 / gsum
    base = N_GROUPS + gsel * EXPERTS_PER_GROUP
    in_g = jnp.logical_and(lane >= base, lane < base + EXPERTS_PER_GROUP)
    el = jnp.where(in_g, lg, neg)
    emax = jnp.max(el, axis=1, keepdims=True)
    ee = jnp.where(in_g, jnp.exp(lg - emax), 0.0)
    prob = ee / jnp.sum(ee, axis=1, keepdims=True)
    pm = jnp.where(in_g, prob, -1.0)
    p1 = jnp.max(pm, axis=1, keepdims=True)
    i1 = jnp.min(jnp.where(pm == p1, lane, big), axis=1, keepdims=True)
    pm2 = jnp.where(lane == i1, -1.0, pm)
    p2 = jnp.max(pm2, axis=1, keepdims=True)
    i2 = jnp.min(jnp.where(pm2 == p2, lane, big), axis=1, keepdims=True)
    tot = p1 + p2
    idx_ref[...] = jnp.where(lane == 0, i1 - N_GROUPS,
                             jnp.where(lane == 1, i2 - N_GROUPS, 0.0)).astype(jnp.int32)
    wt_ref[...] = jnp.where(lane == 0, g_w * (p1 / tot), jnp.where(lane == 1, g_w * (p2 / tot), 0.0))


def moe_router(logits, *, tm=512):
    m = logits.shape[0]
    tm = min(tm, m)
    assert m % tm == 0
    blk = pl.BlockSpec((tm, LANE), lambda i: (i, 0))
    return pl.pallas_call(
        _router_body, grid=(m // tm,), in_specs=[blk], out_specs=[blk, blk],
        out_shape=[jax.ShapeDtypeStruct((m, LANE), jnp.int32), jax.ShapeDtypeStruct((m, LANE), F32)],
        compiler_params=_cparams(("parallel",)), name="moe_router")(logits)


def _experts_body(be_ref, nused_ref, tok_ref, x_hbm, wg_ref, wu_ref, wd_ref, y_ref, xbuf, sem):
    i = pl.program_id(0)
    n_used = nused_ref[0]

    def gather(blk, buf):
        base = blk * EXPERT_BLOCK

        def issue(r, carry):
            pltpu.make_async_copy(x_hbm.at[pl.ds(tok_ref[base + r], 1)],
                                  xbuf.at[buf, pl.ds(r, 1)], sem.at[buf]).start()
            return carry

        lax.fori_loop(0, EXPERT_BLOCK, issue, 0)

    @pl.when(i == 0)
    def _():
        gather(0, 0)

    @pl.when(i < n_used)
    def _():
        buf = i % 2

        @pl.when(i + 1 < n_used)
        def _():
            gather(i + 1, 1 - buf)

        pltpu.make_async_copy(x_hbm.at[pl.ds(0, EXPERT_BLOCK)], xbuf.at[buf], sem.at[buf]).wait()
        x = xbuf[buf].astype(BF16)
        hb = jax.nn.silu(_dot(x, wg_ref[...])) * _dot(x, wu_ref[...])
        y_ref[...] = _dot(hb.astype(BF16), wd_ref[...])


def moe_experts(x_pad, slot_tok, wg, wu, wd, block_expert, n_used, n_blocks):
    d = x_pad.shape[1]
    de = wg.shape[2]
    rows = lambda i, be, nu, tk: (jnp.minimum(i, nu[0] - 1), 0)
    expert = lambda i, be, nu, tk: (be[i], 0, 0)
    grid_spec = pltpu.PrefetchScalarGridSpec(
        num_scalar_prefetch=3, grid=(n_blocks,),
        in_specs=[pl.BlockSpec(memory_space=pl.ANY),
                  pl.BlockSpec((None, d, de), expert),
                  pl.BlockSpec((None, d, de), expert),
                  pl.BlockSpec((None, de, d), expert)],
        out_specs=pl.BlockSpec((EXPERT_BLOCK, d), rows),
        scratch_shapes=[pltpu.VMEM((2, EXPERT_BLOCK, d), F32), pltpu.SemaphoreType.DMA((2,))])
    return pl.pallas_call(
        _experts_body, grid_spec=grid_spec,
        out_shape=jax.ShapeDtypeStruct((n_blocks * EXPERT_BLOCK, d), F32),
        compiler_params=_cparams(("arbitrary",)), name="moe_experts")(
            block_expert, n_used, slot_tok, x_pad, wg, wu, wd)


def _combine_body(slot_ref, yb_hbm, h_ref, w_ref, y_ref, g_s, sem, *, tm):
    i = pl.program_id(0)

    def issue(r, carry):
        tok = i * tm + r
        for kk in range(TOP_K):
            pltpu.make_async_copy(yb_hbm.at[pl.ds(slot_ref[TOP_K * tok + kk], 1)],
                                  g_s.at[kk, pl.ds(r, 1)], sem).start()
        return carry

    lax.fori_loop(0, tm, issue, 0)
    for kk in range(TOP_K):
        pltpu.make_async_copy(yb_hbm.at[pl.ds(0, tm)], g_s.at[kk], sem).wait()
    w = w_ref[...]
    y_ref[...] = h_ref[...] + (g_s[0] * w[:, 0:1] + g_s[1] * w[:, 1:2])


def moe_combine(yb, h, wts, slots, *, tm=128):
    m, d = h.shape
    tm = min(tm, m)
    assert m % tm == 0
    grid_spec = pltpu.PrefetchScalarGridSpec(
        num_scalar_prefetch=1, grid=(m // tm,),
        in_specs=[pl.BlockSpec(memory_space=pl.ANY),
                  pl.BlockSpec((tm, d), lambda i, s: (i, 0)),
                  pl.BlockSpec((tm, LANE), lambda i, s: (i, 0))],
        out_specs=pl.BlockSpec((tm, d), lambda i, s: (i, 0)),
        scratch_shapes=[pltpu.VMEM((TOP_K, tm, d), F32), pltpu.SemaphoreType.DMA(())])
    return pl.pallas_call(
        functools.partial(_combine_body, tm=tm), grid_spec=grid_spec,
        out_shape=jax.ShapeDtypeStruct((m, d), F32),
        compiler_params=_cparams(("arbitrary",)), name="moe_combine")(slots, yb, h, wts)


def moe_dispatch_tables(expert_idx, n_tok):
    n_assign = n_tok * TOP_K
    flat_e = expert_idx.reshape(n_assign).astype(jnp.int32)
    flat_tok = jnp.repeat(jnp.arange(n_tok, dtype=jnp.int32), TOP_K)
    order = jnp.argsort(flat_e)
    e_sorted = flat_e[order]
    counts = jnp.zeros((N_EXPERTS,), jnp.int32).at[flat_e].add(1)
    padded = (counts + EXPERT_BLOCK - 1) // EXPERT_BLOCK * EXPERT_BLOCK
    pad_end = jnp.cumsum(padded)
    pad_start = pad_end - padded
    start = jnp.cumsum(counts) - counts
    dest = pad_start[e_sorted] + jnp.arange(n_assign, dtype=jnp.int32) - start[e_sorted]
    n_blocks = -(-n_assign // EXPERT_BLOCK) + N_EXPERTS
    n_slots = n_blocks * EXPERT_BLOCK
    slot_tok = jnp.full((n_slots,), n_tok, jnp.int32).at[dest].set(flat_tok[order])
    block_start = jnp.arange(n_blocks, dtype=jnp.int32) * EXPERT_BLOCK
    block_expert = jnp.minimum(jnp.searchsorted(pad_end, block_start, side='right'),
                               N_EXPERTS - 1).astype(jnp.int32)
    slots = jnp.zeros((n_assign,), jnp.int32).at[order].set(dest)
    n_used = (pad_end[-1:] // EXPERT_BLOCK).astype(jnp.int32)
    return slot_tok, block_expert, slots, n_used, n_blocks


def _pad_cols(a, width):
    return jnp.pad(a, ((0, 0), (0, width - a.shape[1])))


def _pad_rows(a, height):
    return jnp.pad(a, ((0, height - a.shape[0]), (0, 0)))


def _rwkv_cols_to_slots(a, lora_w, lora_a):
    W = RWKV_WIDTH
    return jnp.concatenate([a[:, :3 * W], _pad_cols(a[:, 3 * W:3 * W + lora_w], LANE),
                            _pad_cols(a[:, 3 * W + lora_w:3 * W + lora_w + lora_a], LANE),
                            a[:, 3 * W + lora_w + lora_a:]], axis=1)


def _rwkv_slots_to_cols(a, lora_w, lora_a):
    W = RWKV_WIDTH
    return jnp.concatenate([a[:, :3 * W], a[:, 3 * W:3 * W + lora_w],
                            a[:, 3 * W + LANE:3 * W + LANE + lora_a], a[:, 3 * W + 2 * LANE:]], axis=1)


def kernel(x_prompt, x_sample, mem_prompt, cache_fox_k, cache_fox_v, cache_fox_logf, page_table, state_rwkv, state_rwkv_shift, cache_mem_k, cache_mem_v, norm_mix, w_in, fox_q_gain, fox_k_gain, fox_b_f, rwkv_mu, rwkv_w0, rwkv_w2, rwkv_a0, rwkv_a2, rwkv_g2, rwkv_k_k, rwkv_k_a, rwkv_r_k, rwkv_gn_w, rwkv_gn_b, w_out, norm_mem_q, norm_mem_src, w_mem_q, w_mem_kv, mem_q_gain, mem_k_gain, w_mem_o, norm_ffn, w_router_group, w_router_expert, w_exp_gate, w_exp_up, w_exp_down):
    n_b, seq, d_model = x_prompt.shape
    n_dec, dec_seq, _ = x_sample.shape
    assert n_b == 1
    n_p = n_b * seq
    n_s = n_dec * dec_seq
    n_tok = n_p + n_s
    lora_w = rwkv_w2.shape[0]
    lora_a = rwkv_a2.shape[0]
    W = RWKV_WIDTH

    x = jnp.concatenate([x_prompt.reshape(n_p, d_model), x_sample.reshape(n_s, d_model)], axis=0)

    w_in_b = w_in.astype(BF16)
    fox_w, rwkv_w = w_in_b[:, :3 * FOX_WIDTH + FOX_HEADS], w_in_b[:, 3 * FOX_WIDTH + FOX_HEADS:]
    w1 = jnp.concatenate([fox_w[:, :3 * FOX_WIDTH], _rwkv_cols_to_slots(rwkv_w, lora_w, lora_a),
                          fox_w[:, 3 * FOX_WIDTH:],
                          jnp.zeros((d_model, PROJ_COLS_PADDED - COL_F - FOX_HEADS), BF16)], axis=1)
    head_gain = jnp.concatenate([jnp.tile(fox_q_gain, FOX_HEADS), jnp.tile(fox_k_gain, FOX_HEADS),
                                 jnp.ones((PROJ_COLS_PADDED - 2 * FOX_WIDTH,), F32)])
    proj = fused_matmul([x], [w1], norm_gain=norm_mix, head_gain=head_gain,
                        n_head_tiles=2 * FOX_WIDTH // PROJ_TN, tn=PROJ_TN, name="proj_in")

    logf_p, c_p = fox_logf_cumsum(proj, 0, n_p, fox_b_f)
    logf_s, _ = fox_logf_cumsum(proj, n_p, n_s, fox_b_f)
    fox_out_p = fox_prompt_attention(proj, c_p, seq)
    fox_out_s = fox_decode_attention(proj, n_p, logf_s, cache_fox_k, cache_fox_v, cache_fox_logf,
                                     page_table, dec_seq)
    fox_out = jnp.concatenate([fox_out_p, fox_out_s], axis=0)

    p_rwkv = proj[:, COL_RWKV:COL_RWKV + RWKV_COLS_PADDED]
    shift_s = _rwkv_cols_to_slots(state_rwkv_shift, lora_w, lora_a)
    p_s = p_rwkv[n_p:].reshape(n_dec, dec_seq, RWKV_COLS_PADDED)
    short_first = jnp.zeros((n_dec, dec_seq, RWKV_COLS_PADDED), F32).at[:, 0].set(shift_s)
    mu = _rwkv_cols_to_slots(rwkv_mu.reshape(1, -1), lora_w, lora_a)
    r, dec, k2, v, a_vec, b_vec, g = rwkv_prep(
        proj, n_p, short_first.reshape(n_s, RWKV_COLS_PADDED), dec_seq, mu, rwkv_w0,
        _pad_rows(rwkv_w2, LANE).astype(BF16), rwkv_a0,
        _pad_rows(rwkv_a2, LANE).astype(BF16), rwkv_g2.astype(BF16), rwkv_k_k, rwkv_k_a)
    seqs = (r, dec, k2, v, a_vec, b_vec)
    s0_p = jnp.zeros((n_b, RWKV_HEAD_DIM, W), F32)
    s0_s = state_rwkv.astype(F32).transpose(0, 2, 1, 3).reshape(n_dec, RWKV_HEAD_DIM, W)
    o_p, st_p = rwkv_scan(seqs, 0, n_b, seq, s0_p)
    o_s, st_s = rwkv_scan(seqs, n_p, n_dec, dec_seq, s0_s, n_par=8 if n_dec % 8 == 0 else 1,
                          unroll=1)
    o = jnp.concatenate([o_p, o_s], axis=0)
    rwkv_out = rwkv_post(o, r, k2, v, g, rwkv_r_k, rwkv_gn_w, rwkv_gn_b)
    unstate = lambda s: s.reshape(-1, RWKV_HEAD_DIM, RWKV_HEADS, RWKV_HEAD_DIM).transpose(0, 2, 1, 3)

    w_out_b = w_out.astype(BF16)
    h1 = fused_matmul([fox_out, rwkv_out], [w_out_b[:FOX_WIDTH], w_out_b[FOX_WIDTH:]], residual=x,
                      name="proj_out")

    mem_kv = fused_matmul([mem_prompt.reshape(-1, d_model)], [w_mem_kv.astype(BF16)],
                          norm_gain=norm_mem_src,
                          head_gain=jnp.concatenate([jnp.tile(mem_k_gain, MEM_HEADS),
                                                     jnp.ones((MEM_WIDTH,), F32)]),
                          n_head_tiles=1, tn=MEM_WIDTH, name="mem_kv")
    n_mem = mem_prompt.shape[1]
    mem_k_p = mem_kv[:, :MEM_WIDTH].reshape(n_b, n_mem, MEM_WIDTH)
    mem_v_p = mem_kv[:, MEM_WIDTH:].reshape(n_b, n_mem, MEM_WIDTH)
    q_mem = fused_matmul([h1], [w_mem_q.astype(BF16)], norm_gain=norm_mem_q,
                         head_gain=jnp.tile(mem_q_gain, MEM_HEADS), n_head_tiles=1, tn=MEM_WIDTH,
                         name="mem_q")
    att_p = memory_attention(q_mem, 0, n_b, seq, mem_k_p, mem_v_p)
    att_s = memory_attention(q_mem, n_p, n_dec, dec_seq,
                             cache_mem_k.reshape(n_dec, -1, MEM_WIDTH),
                             cache_mem_v.reshape(n_dec, -1, MEM_WIDTH))
    h2 = fused_matmul([jnp.concatenate([att_p, att_s], axis=0)], [w_mem_o.astype(BF16)],
                      residual=h1, name="mem_out")

    w_router = _pad_cols(jnp.concatenate([w_router_group, w_router_expert], axis=1), LANE)
    logits, xn = fused_matmul([h2], [w_router], norm_gain=norm_ffn, emit_lhs=True, exact=True,
                              tn=LANE, name="moe_logits")
    idx, wts = moe_router(logits)
    slot_tok, block_expert, slots, n_used, n_blocks = moe_dispatch_tables(idx[:, :TOP_K], n_tok)
    xn_pad = jnp.concatenate([xn, jnp.zeros((8, d_model), F32)], axis=0)
    yb = moe_experts(xn_pad, slot_tok, w_exp_gate.astype(BF16), w_exp_up.astype(BF16),
                     w_exp_down.astype(BF16), block_expert, n_used, n_blocks)
    y = moe_combine(yb, h2, wts, slots)

    heads = lambda t, b, s: t.reshape(b, s, FOX_HEADS, FOX_HEAD_DIM)
    p_rwkv_last_p = _rwkv_slots_to_cols(p_rwkv[n_p - 1:n_p], lora_w, lora_a)
    p_rwkv_last_s = _rwkv_slots_to_cols(p_s[:, -1], lora_w, lora_a)
    return (y[:n_p].reshape(n_b, seq, d_model), y[n_p:].reshape(n_dec, dec_seq, d_model),
            heads(proj[:n_p, COL_K:COL_K + FOX_WIDTH], n_b, seq),
            heads(proj[:n_p, COL_V:COL_V + FOX_WIDTH], n_b, seq),
            logf_p.reshape(n_b, seq, FOX_HEADS),
            unstate(st_p), p_rwkv_last_p,
            mem_k_p.reshape(n_b, n_mem, MEM_HEADS, MEM_HEAD_DIM),
            mem_v_p.reshape(n_b, n_mem, MEM_HEADS, MEM_HEAD_DIM),
            heads(proj[n_p:, COL_K:COL_K + FOX_WIDTH], n_dec, dec_seq),
            heads(proj[n_p:, COL_V:COL_V + FOX_WIDTH], n_dec, dec_seq),
            logf_s.reshape(n_dec, dec_seq, FOX_HEADS),
            unstate(st_s), p_rwkv_last_s)
```
